```python
import jax, jax.numpy as jnp
from jax import lax
import numpy as np

D_MODEL = 1024
BATCH = 2
SEQ = 16384
DEPTH = 4

N_MIXERS = 2
N_HEADS = 16
HEAD_DIM = D_MODEL // N_HEADS
Q_BLOCK = 128
SGU_CHUNK = 128
SGU_WIDTH = 2 * D_MODEL
SGU_GROUPS = 16
SGU_GROUP_DIM = SGU_WIDTH // SGU_GROUPS
FFN_HIDDEN = ((8 * D_MODEL // 3 + 255) // 256) * 256
N_ATTN_LAYERS = (DEPTH + 1) // 2
N_SGU_LAYERS = DEPTH // 2
NORM_EPS = 1e-6
LN_EPS = 1e-5

kernel_name = "fox_gmlp_interleaved_hybrid"


def rms_norm(x, w):
    xf = x.astype(jnp.float32)
    y = xf * lax.rsqrt(jnp.mean(xf * xf, axis=-1, keepdims=True) + NORM_EPS)
    return (y * w.astype(jnp.float32)).astype(x.dtype)


def forgetting_attention(h, w_in, b_f, w_out):
    B, S, _ = h.shape
    proj = h @ w_in
    q, k, v, f_logit = jnp.split(proj, [D_MODEL, 2 * D_MODEL, 3 * D_MODEL], axis=-1)
    q = q.reshape(B, S, N_HEADS, HEAD_DIM).transpose(0, 2, 1, 3)
    k = k.reshape(B, S, N_HEADS, HEAD_DIM).transpose(0, 2, 1, 3)
    v = v.reshape(B, S, N_HEADS, HEAD_DIM).transpose(0, 2, 1, 3)
    log_f = jax.nn.log_sigmoid((f_logit + b_f).astype(jnp.float32))
    c = jnp.cumsum(log_f, axis=1).transpose(0, 2, 1)
    n_blk = S // Q_BLOCK
    q_blk = q.reshape(B, N_HEADS, n_blk, Q_BLOCK, HEAD_DIM).transpose(2, 0, 1, 3, 4)
    c_blk = c.reshape(B, N_HEADS, n_blk, Q_BLOCK).transpose(2, 0, 1, 3)
    k_pos = jnp.arange(S)
    scale = HEAD_DIM ** -0.5

    def attend(args):
        qb, cb, start = args
        s = jnp.einsum('bhqd,bhkd->bhqk', qb, k, preferred_element_type=jnp.float32) * scale
        s = s + cb[..., :, None] - c[:, :, None, :]
        q_pos = start + jnp.arange(Q_BLOCK)
        s = jnp.where(k_pos[None, :] <= q_pos[:, None], s, -jnp.inf)
        p = jax.nn.softmax(s, axis=-1).astype(v.dtype)
        return jnp.einsum('bhqk,bhkd->bhqd', p, v)

    o = lax.map(attend, (q_blk, c_blk, jnp.arange(n_blk) * Q_BLOCK))
    o = o.transpose(1, 0, 3, 2, 4).reshape(B, S, D_MODEL)
    return o @ w_out


def spatial_gating_mixer(h, w_in, ln_g, ln_b, w_s, b_s, w_out):
    B, S, _ = h.shape
    z = jax.nn.gelu(h @ w_in, approximate=False)
    u, v = jnp.split(z, 2, axis=-1)
    vf = v.astype(jnp.float32)
    mu = jnp.mean(vf, axis=-1, keepdims=True)
    var = jnp.mean(jnp.square(vf - mu), axis=-1, keepdims=True)
    vn = ((vf - mu) * lax.rsqrt(var + LN_EPS) * ln_g.astype(jnp.float32)
          + ln_b.astype(jnp.float32)).astype(v.dtype)
    vc = vn.reshape(B, S // SGU_CHUNK, SGU_CHUNK, SGU_GROUPS, SGU_GROUP_DIM)
    w_causal = jnp.tril(w_s)
    mixed = jnp.einsum('gts,bcsgd->bctgd', w_causal, vc) + b_s.T[:, :, None]
    gated = u * mixed.reshape(B, S, SGU_WIDTH)
    return gated @ w_out


def swiglu_ffn(h, w_in, w_out):
    g, u = jnp.split(h @ w_in, 2, axis=-1)
    return (jax.nn.silu(g) * u) @ w_out


def setup_inputs(seed: int = 0) -> dict:
    key = jax.random.key(seed)
    ks = jax.random.split(key, 20)
    f32 = jnp.float32
    D = D_MODEL

    def nrm(k, shape, fan_in):
        return jax.random.normal(k, shape, f32) * (fan_in ** -0.5)

    def gain(k, shape):
        return 1.0 + 0.05 * jax.random.normal(k, shape, f32)

    x = jax.random.normal(ks[0], (BATCH, SEQ, D), f32)
    mixer_norm_w = gain(ks[1], (DEPTH, D))
    attn_w_in = nrm(ks[2], (N_ATTN_LAYERS, D, 3 * D + N_HEADS), D)
    attn_b_f = jax.random.uniform(ks[3], (N_ATTN_LAYERS, N_HEADS), f32, 1.0, 6.0)
    attn_w_out = nrm(ks[4], (N_ATTN_LAYERS, D, D), D)
    sgu_w_in = nrm(ks[5], (N_SGU_LAYERS, D, 2 * SGU_WIDTH), D)
    sgu_ln_g = gain(ks[6], (N_SGU_LAYERS, SGU_WIDTH))
    sgu_ln_b = 0.02 * jax.random.normal(ks[7], (N_SGU_LAYERS, SGU_WIDTH), f32)
    sgu_w_s = nrm(ks[8], (N_SGU_LAYERS, SGU_GROUPS, SGU_CHUNK, SGU_CHUNK), SGU_CHUNK)
    sgu_b_s = 1.0 + 0.1 * jax.random.normal(ks[9], (N_SGU_LAYERS, SGU_GROUPS, SGU_CHUNK), f32)
    sgu_w_out = nrm(ks[10], (N_SGU_LAYERS, SGU_WIDTH, D), SGU_WIDTH)
    ffn_norm_w = gain(ks[11], (DEPTH, D))
    ffn_w_in = nrm(ks[12], (DEPTH, D, 2 * FFN_HIDDEN), D)
    ffn_w_out = nrm(ks[13], (DEPTH, FFN_HIDDEN, D), FFN_HIDDEN)
    final_norm_w = gain(ks[14], (D,))
    return {"x": x, "mixer_norm_w": mixer_norm_w, "attn_w_in": attn_w_in,
            "attn_b_f": attn_b_f, "attn_w_out": attn_w_out, "sgu_w_in": sgu_w_in,
            "sgu_ln_g": sgu_ln_g, "sgu_ln_b": sgu_ln_b, "sgu_w_s": sgu_w_s,
            "sgu_b_s": sgu_b_s, "sgu_w_out": sgu_w_out, "ffn_norm_w": ffn_norm_w,
            "ffn_w_in": ffn_w_in, "ffn_w_out": ffn_w_out, "final_norm_w": final_norm_w}


def reference(x, mixer_norm_w, attn_w_in, attn_b_f, attn_w_out, sgu_w_in, sgu_ln_g,
              sgu_ln_b, sgu_w_s, sgu_b_s, sgu_w_out, ffn_norm_w, ffn_w_in, ffn_w_out,
              final_norm_w):
    for i in range(DEPTH):
        h = rms_norm(x, mixer_norm_w[i])
        j = i // N_MIXERS
        if i % N_MIXERS == 0:
            x = x + forgetting_attention(h, attn_w_in[j], attn_b_f[j], attn_w_out[j])
        else:
            x = x + spatial_gating_mixer(h, sgu_w_in[j], sgu_ln_g[j], sgu_ln_b[j],
                                         sgu_w_s[j], sgu_b_s[j], sgu_w_out[j])
        x = x + swiglu_ffn(rms_norm(x, ffn_norm_w[i]), ffn_w_in[i], ffn_w_out[i])
    return rms_norm(x, final_norm_w)
```

```python
import functools
import math

import jax
import jax.numpy as jnp
from jax import lax
from jax.experimental import pallas as pl
from jax.experimental.pallas import tpu as pltpu

N_HEADS = 16
SGU_CHUNK = 128
SGU_GROUPS = 16
NORM_EPS = 1e-6
LN_EPS = 1e-5

LANES = 128
V7X_VMEM_BYTES = 64 * 1024 * 1024
VMEM_LIMIT_CAP = 60000 * 1024

ROW_TILE = 512
ATTN_TILE = 512
FFN_CHUNK = 256
SGU_COL_CHUNK = 512
CUMSUM_ROWS = 8

F32 = jnp.float32
BF16 = jnp.bfloat16


def _vmem_limit(nbytes):
    return int(min(VMEM_LIMIT_CAP, max(16 * 1024 * 1024, nbytes * 3 // 2)))


def _resident(shape):
    zeros = (0,) * len(shape)
    return pl.BlockSpec(shape, lambda *_: zeros, pipeline_mode=pl.Buffered(1))


def _dot(a, b):
    return jnp.dot(a, b, preferred_element_type=F32)


def _rms_norm(xf, w):
    ms = jnp.mean(xf * xf, axis=-1, keepdims=True)
    return xf * lax.rsqrt(ms + NORM_EPS) * w


def _gelu(z):
    return 0.5 * z * (1.0 + lax.erf(z * (1.0 / math.sqrt(2.0))))


def _attn_proj_kernel(x_ref, nw_ref, wqkv_ref, wf_ref, bf_ref, q_ref, k_ref, v_ref, lf_ref,
                      *, d_model, scale):
    h = _rms_norm(x_ref[...], nw_ref[...])
    hb = h.astype(BF16)
    q_ref[...] = (_dot(hb, wqkv_ref[:, 0:d_model]) * scale).astype(BF16)
    k_ref[...] = _dot(hb, wqkv_ref[:, d_model:2 * d_model]).astype(BF16)
    v_ref[...] = _dot(hb, wqkv_ref[:, 2 * d_model:3 * d_model]).astype(BF16)
    h_lo = (h - hb.astype(F32)).astype(BF16)
    hcat = jnp.concatenate([hb, h_lo, hb], axis=1)
    z = _dot(hcat, wf_ref[...]) + bf_ref[...]
    logf = jnp.minimum(z, 0.0) - jnp.log1p(jnp.exp(-jnp.abs(z)))
    lf_ref[...] = logf[:, :lf_ref.shape[1]]


def _attn_proj(x2, norm_w, w_in, b_f, *, tm):
    n, d = x2.shape
    h = N_HEADS
    scale = float(d // h) ** -0.5
    wqkv = w_in[:, :3 * d].astype(BF16)
    wf = jnp.pad(w_in[:, 3 * d:], ((0, 0), (0, LANES - h)))
    wf_hi = wf.astype(BF16)
    wf_lo = (wf - wf_hi.astype(F32)).astype(BF16)
    wf_cat = jnp.concatenate([wf_hi, wf_hi, wf_lo], axis=0)
    bf = jnp.pad(b_f, (0, LANES - h)).reshape(1, LANES)
    est = (2 * tm * d * 4 + wqkv.size * 2 + wf_cat.size * 2 + 3 * 2 * tm * d * 2
           + 2 * tm * LANES * 4 + 6 * tm * d * 4)
    row = lambda i: (i, 0)
    return pl.pallas_call(
        functools.partial(_attn_proj_kernel, d_model=d, scale=scale),
        grid=(n // tm,),
        in_specs=[pl.BlockSpec((tm, d), row), _resident((1, d)), _resident(wqkv.shape),
                  _resident(wf_cat.shape), _resident((1, LANES))],
        out_specs=[pl.BlockSpec((tm, d), row)] * 3 + [pl.BlockSpec((tm, h), row)],
        out_shape=[jax.ShapeDtypeStruct((n, d), BF16)] * 3 + [jax.ShapeDtypeStruct((n, h), F32)],
        compiler_params=pltpu.CompilerParams(dimension_semantics=("parallel",),
                                             vmem_limit_bytes=_vmem_limit(est)),
        name="attn_proj",
    )(x2, norm_w.reshape(1, d), wqkv, wf_cat, bf)


def _split3(x):
    hi = x.astype(BF16)
    r1 = x - hi.astype(F32)
    mid = r1.astype(BF16)
    lo = (r1 - mid.astype(F32)).astype(BF16)
    return hi, mid, lo


def _cumsum_kernel(x_ref, c_ref):
    rows = x_ref.shape[1]
    r_i = lax.broadcasted_iota(jnp.int32, (LANES, LANES), 0)
    c_i = lax.broadcasted_iota(jnp.int32, (LANES, LANES), 1)
    upper = (r_i <= c_i).astype(BF16)
    rr_i = lax.broadcasted_iota(jnp.int32, (rows, rows), 0)
    rc_i = lax.broadcasted_iota(jnp.int32, (rows, rows), 1)
    strict_lower = (rc_i < rr_i).astype(BF16)
    upper3 = jnp.concatenate([upper] * 3, axis=0)
    lower3 = jnp.concatenate([strict_lower] * 3, axis=1)
    for g in range(x_ref.shape[0]):
        hi, mid, lo = _split3(x_ref[g])
        inner = _dot(jnp.concatenate([hi, mid, lo], axis=1), upper3)
        prev = _dot(lower3, jnp.concatenate([hi, mid, lo], axis=0))
        c_ref[g] = inner + jnp.sum(prev, axis=1, keepdims=True)


def _cumsum_rows(lf_rows):
    r, rows, lanes = lf_rows.shape
    g = CUMSUM_ROWS
    blk = pl.BlockSpec((g, rows, lanes), lambda i: (i, 0, 0))
    return pl.pallas_call(
        _cumsum_kernel,
        grid=(r // g,),
        in_specs=[blk],
        out_specs=blk,
        out_shape=jax.ShapeDtypeStruct(lf_rows.shape, F32),
        compiler_params=pltpu.CompilerParams(dimension_semantics=("parallel",)),
        name="logf_cumsum",
    )(lf_rows)


def _attn_kernel(q_ref, k_ref, v_ref, c_ref, o_ref, m_scr, l_scr, acc_scr, *, blk, dh):
    i = pl.program_id(2)
    q = q_ref[...]
    lane = lax.broadcasted_iota(jnp.int32, q.shape, 1)
    zero = jnp.zeros_like(q)
    q_heads = (jnp.where(lane < dh, q, zero), jnp.where(lane >= dh, q, zero))
    c_tile = [c_ref[hh, i][:, 0:1] for hh in range(2)]

    m_scr[...] = jnp.full(m_scr.shape, -jnp.inf, F32)
    l_scr[...] = jnp.zeros(l_scr.shape, F32)
    acc_scr[...] = jnp.zeros(acc_scr.shape, F32)

    def block(j, masked):
        start = pl.multiple_of(j * blk, blk)
        kb = k_ref[pl.ds(start, blk), :]
        vb = v_ref[pl.ds(start, blk), :]
        for hh in range(2):
            s = lax.dot_general(q_heads[hh], kb, (((1,), (1,)), ((), ())),
                                preferred_element_type=F32)
            s = s + (c_tile[hh] - c_ref[hh, j])
            if masked:
                r_i = lax.broadcasted_iota(jnp.int32, s.shape, 0)
                c_i = lax.broadcasted_iota(jnp.int32, s.shape, 1)
                s = jnp.where(c_i <= r_i, s, -jnp.inf)
            m_prev = m_scr[hh]
            m_new = jnp.maximum(m_prev, jnp.max(s, axis=1, keepdims=True))
            alpha = jnp.exp(m_prev - m_new)
            p = jnp.exp(s - m_new)
            l_scr[hh] = alpha * l_scr[hh] + jnp.sum(p, axis=1, keepdims=True)
            acc_scr[hh] = alpha * acc_scr[hh] + _dot(p.astype(BF16), vb)
            m_scr[hh] = m_new

    def body(j, carry):
        block(j, masked=False)
        return carry

    lax.fori_loop(0, i, body, 0)
    block(i, masked=True)

    out0 = acc_scr[0] / l_scr[0]
    out1 = acc_scr[1] / l_scr[1]
    o_ref[...] = jnp.where(lane < dh, out0, out1).astype(o_ref.dtype)


def _attention(q, k, v, c, *, batch, seq):
    n, d = q.shape
    blk = ATTN_TILE
    dh = d // N_HEADS
    pair = 2 * dh
    assert pair == LANES and seq % blk == 0
    nblk = seq // blk
    q3, k3, v3 = (a.reshape(batch, seq, d) for a in (q, k, v))
    c5 = c.reshape(batch, N_HEADS, nblk, 1, blk)
    est = (2 * 2 * seq * pair * 2 + 2 * 2 * seq * 8 * 4 + 4 * blk * pair * 2
           + 2 * blk * pair * 4 + 4 * blk * LANES * 4 + 8 * blk * blk * 4)
    out = pl.pallas_call(
        functools.partial(_attn_kernel, blk=blk, dh=dh),
        grid=(batch, N_HEADS // 2, nblk),
        in_specs=[pl.BlockSpec((None, blk, pair), lambda b, hp, i: (b, i, hp)),
                  pl.BlockSpec((None, seq, pair), lambda b, hp, i: (b, 0, hp)),
                  pl.BlockSpec((None, seq, pair), lambda b, hp, i: (b, 0, hp)),
                  pl.BlockSpec((None, 2, nblk, 1, blk), lambda b, hp, i: (b, hp, 0, 0, 0))],
        out_specs=pl.BlockSpec((None, blk, pair), lambda b, hp, i: (b, i, hp)),
        out_shape=jax.ShapeDtypeStruct((batch, seq, d), BF16),
        scratch_shapes=[pltpu.VMEM((2, blk, 1), F32), pltpu.VMEM((2, blk, 1), F32),
                        pltpu.VMEM((2, blk, pair), F32)],
        compiler_params=pltpu.CompilerParams(
            dimension_semantics=("parallel", "parallel", "arbitrary"),
            vmem_limit_bytes=_vmem_limit(est)),
        name="fox_attention",
    )(q3, k3, v3, c5)
    return out.reshape(n, d)


def _sgu_kernel(x_ref, nw_ref, win_ref, lng_ref, lnb_ref, ws_ref, bs_ref, o_ref, v_scr, vn_scr,
                *, width):
    tm = x_ref.shape[0]
    cw = SGU_COL_CHUNK
    n_pos_chunks = tm // SGU_CHUNK
    gdim = width // SGU_GROUPS
    hb = _rms_norm(x_ref[...], nw_ref[...]).astype(BF16)

    for c in range(width // cw):
        v_scr[:, c * cw:(c + 1) * cw] = _gelu(_dot(hb, win_ref[:, width + c * cw:width + (c + 1) * cw]))
    v = v_scr[...]
    mu = jnp.mean(v, axis=-1, keepdims=True)
    dv = v - mu
    var = jnp.mean(dv * dv, axis=-1, keepdims=True)
    vn_scr[...] = (dv * lax.rsqrt(var + LN_EPS) * lng_ref[...] + lnb_ref[...]).astype(BF16)

    r_i = lax.broadcasted_iota(jnp.int32, (SGU_CHUNK, SGU_CHUNK), 0)
    c_i = lax.broadcasted_iota(jnp.int32, (SGU_CHUNK, SGU_CHUNK), 1)
    causal = c_i <= r_i
    for c in range(width // cw):
        u = _gelu(_dot(hb, win_ref[:, c * cw:(c + 1) * cw]))
        for gg in range(cw // gdim):
            g = c * (cw // gdim) + gg
            w_tril = jnp.where(causal, ws_ref[g], 0.0).astype(BF16)
            cols = slice(g * gdim, (g + 1) * gdim)
            v_cat = jnp.concatenate(
                [vn_scr[pc * SGU_CHUNK:(pc + 1) * SGU_CHUNK, cols] for pc in range(n_pos_chunks)],
                axis=1)
            mixed = _dot(w_tril, v_cat) + bs_ref[g]
            for pc in range(n_pos_chunks):
                rows = slice(pc * SGU_CHUNK, (pc + 1) * SGU_CHUNK)
                gated = u[rows, gg * gdim:(gg + 1) * gdim] * mixed[:, pc * gdim:(pc + 1) * gdim]
                o_ref[rows, cols] = gated.astype(o_ref.dtype)


def _sgu(x2, norm_w, w_in, ln_g, ln_b, w_s, b_s, *, tm):
    n, d = x2.shape
    width = w_in.shape[1] // 2
    assert width // SGU_GROUPS == LANES and w_s.shape[1] == SGU_CHUNK and tm % SGU_CHUNK == 0
    win = w_in.astype(BF16)
    est = (2 * tm * d * 4 + win.size * 2 + 2 * w_s.size * 4 + 2 * tm * width * 2
           + tm * width * 4 + tm * width * 2 + 4 * tm * width * 4)
    row = lambda i: (i, 0)
    return pl.pallas_call(
        functools.partial(_sgu_kernel, width=width),
        grid=(n // tm,),
        in_specs=[pl.BlockSpec((tm, d), row), _resident((1, d)), _resident(win.shape),
                  _resident((1, width)), _resident((1, width)), _resident(w_s.shape),
                  _resident((SGU_GROUPS, SGU_CHUNK, 1))],
        out_specs=pl.BlockSpec((tm, width), row),
        out_shape=jax.ShapeDtypeStruct((n, width), BF16),
        scratch_shapes=[pltpu.VMEM((tm, width), F32), pltpu.VMEM((tm, width), BF16)],
        compiler_params=pltpu.CompilerParams(dimension_semantics=("parallel",),
                                             vmem_limit_bytes=_vmem_limit(est)),
        name="sgu_gate",
    )(x2, norm_w.reshape(1, d), win, ln_g.reshape(1, width), ln_b.reshape(1, width), w_s,
      b_s.reshape(SGU_GROUPS, SGU_CHUNK, 1))


def _post_kernel(x_ref, a_ref, wa_ref, nw_ref, win_ref, wout_ref, fw_ref, o_ref, *, hidden, final):
    xm = x_ref[...] + _dot(a_ref[...], wa_ref[...])
    hb = _rms_norm(xm, nw_ref[...]).astype(BF16)
    o_ref[...] = xm
    th = FFN_CHUNK
    for c in range(hidden // th):
        g = _dot(hb, win_ref[:, c * th:(c + 1) * th])
        u = _dot(hb, win_ref[:, hidden + c * th:hidden + (c + 1) * th])
        act = (g * (1.0 / (1.0 + jnp.exp(-g))) * u).astype(BF16)
        o_ref[...] += _dot(act, wout_ref[c * th:(c + 1) * th, :])
    if final:
        o_ref[...] = _rms_norm(o_ref[...], fw_ref[...])


def _post(x2, a, w_a, norm_w, w_in, w_out, final_w, *, tm, final):
    n, d = x2.shape
    ka = a.shape[1]
    hidden = w_out.shape[0]
    assert hidden % FFN_CHUNK == 0
    wa, win, wout = w_a.astype(BF16), w_in.astype(BF16), w_out.astype(BF16)
    est = (2 * tm * d * 4 + 2 * tm * ka * 2 + (wa.size + win.size + wout.size) * 2
           + 2 * tm * d * 4 + 4 * tm * d * 4)
    row = lambda i: (i, 0)
    return pl.pallas_call(
        functools.partial(_post_kernel, hidden=hidden, final=final),
        grid=(n // tm,),
        in_specs=[pl.BlockSpec((tm, d), row), pl.BlockSpec((tm, ka), row), _resident(wa.shape),
                  _resident((1, d)), _resident(win.shape), _resident(wout.shape),
                  _resident((1, d))],
        out_specs=pl.BlockSpec((tm, d), row),
        out_shape=jax.ShapeDtypeStruct((n, d), F32),
        compiler_params=pltpu.CompilerParams(dimension_semantics=("parallel",),
                                             vmem_limit_bytes=_vmem_limit(est)),
        name="mixer_out_ffn_final" if final else "mixer_out_ffn",
    )(x2, a, wa, norm_w.reshape(1, d), win, wout, final_w.reshape(1, d))


def kernel(x, mixer_norm_w, attn_w_in, attn_b_f, attn_w_out, sgu_w_in, sgu_ln_g, sgu_ln_b,
           sgu_w_s, sgu_b_s, sgu_w_out, ffn_norm_w, ffn_w_in, ffn_w_out, final_norm_w):
    batch, seq, d = x.shape
    n = batch * seq
    depth = mixer_norm_w.shape[0]
    tm = ROW_TILE
    assert n % tm == 0 and seq % ATTN_TILE == 0 and seq % LANES == 0
    x2 = x.reshape(n, d)
    for i in range(depth):
        j = i // 2
        if i % 2 == 0:
            q, k, v, lf = _attn_proj(x2, mixer_norm_w[i], attn_w_in[j], attn_b_f[j], tm=tm)
            lf_rows = lf.reshape(batch, seq, N_HEADS).transpose(0, 2, 1)
            c = _cumsum_rows(lf_rows.reshape(batch * N_HEADS, seq // LANES, LANES))
            a = _attention(q, k, v, c, batch=batch, seq=seq)
            w_a = attn_w_out[j]
        else:
            a = _sgu(x2, mixer_norm_w[i], sgu_w_in[j], sgu_ln_g[j], sgu_ln_b[j], sgu_w_s[j],
                     sgu_b_s[j], tm=tm)
            w_a = sgu_w_out[j]
        x2 = _post(x2, a, w_a, ffn_norm_w[i], ffn_w_in[i], ffn_w_out[i], final_norm_w,
                   tm=tm, final=(i == depth - 1))
    return x2.reshape(batch, seq, d)
```

```python
import functools
import math

import jax
import jax.numpy as jnp
from jax import lax
from jax.experimental import pallas as pl
from jax.experimental.pallas import tpu as pltpu

N_HEADS = 16
SGU_CHUNK = 128
SGU_GROUPS = 16
NORM_EPS = 1e-6
LN_EPS = 1e-5

LANES = 128
BF16_SUBLANES = 16
VMEM_LIMIT_CAP = 60000 * 1024

ROW_TILE = 512
ATTN_TILE = 512
FFN_CHUNK = 256
SGU_COL_CHUNK = 512

LOG2E = math.log2(math.e)
F32 = jnp.float32
BF16 = jnp.bfloat16
NT_DIMS = (((1,), (1,)), ((), ()))


def _vmem_limit(nbytes):
    return int(min(VMEM_LIMIT_CAP, max(16 * 1024 * 1024, nbytes * 3 // 2)))


def _resident(shape):
    zeros = (0,) * len(shape)
    return pl.BlockSpec(shape, lambda *_: zeros, pipeline_mode=pl.Buffered(1))


def _dot(a, b):
    return jnp.dot(a, b, preferred_element_type=F32)


def _rms_norm(xf, w):
    ms = jnp.mean(xf * xf, axis=-1, keepdims=True)
    return xf * lax.rsqrt(ms + NORM_EPS) * w


def _gelu(z):
    return 0.5 * z * (1.0 + lax.erf(z * (1.0 / math.sqrt(2.0))))


def _split3(x):
    hi = x.astype(BF16)
    r1 = x - hi.astype(F32)
    mid = r1.astype(BF16)
    lo = (r1 - mid.astype(F32)).astype(BF16)
    return hi, mid, lo


def _attn_proj_kernel(x_ref, nw_ref, wqT_ref, wk_ref, wvT_ref, wf_ref, bf_ref, route_ref,
                      qT_ref, k_ref, vT_ref, kc_ref, tot_ref, *, q_scale):
    tm = x_ref.shape[0]
    h = _rms_norm(x_ref[...], nw_ref[...])
    hb = h.astype(BF16)
    qT = lax.dot_general(wqT_ref[...], hb, NT_DIMS, preferred_element_type=F32)
    qT_ref[...] = (qT * q_scale).astype(BF16)
    k_ref[...] = _dot(hb, wk_ref[...]).astype(BF16)
    vT = lax.dot_general(wvT_ref[...], hb, NT_DIMS, preferred_element_type=F32)
    vT_ref[...] = vT.astype(BF16)

    h_lo = (h - hb.astype(F32)).astype(BF16)
    hcat = jnp.concatenate([hb, h_lo, hb], axis=1)
    z = _dot(hcat, wf_ref[...]) + bf_ref[...]
    logf = jnp.minimum(z, 0.0) - jnp.log1p(jnp.exp(-jnp.abs(z)))

    r_i = lax.broadcasted_iota(jnp.int32, (tm, tm), 0)
    c_i = lax.broadcasted_iota(jnp.int32, (tm, tm), 1)
    lower = (c_i <= r_i).astype(BF16)
    w3 = _dot(lower, jnp.concatenate(_split3(logf), axis=1))
    w = w3[:, :LANES] + w3[:, LANES:2 * LANES] + w3[:, 2 * LANES:]
    tot_ref[...] = w[tm - 1:tm, :]
    bias = jnp.concatenate(_split3(w * (-LOG2E)), axis=1)
    kc_ref[...] = _dot(bias, route_ref[...]).astype(BF16)


def _kc_route(d):
    part = jnp.arange(3 * LANES) // LANES
    head = jnp.arange(3 * LANES) % LANES
    target = (head // 2) * LANES + 3 * (head % 2) + part
    valid = head < N_HEADS
    cols = jnp.arange(d)
    return ((cols[None, :] == target[:, None]) & valid[:, None]).astype(BF16)


def _attn_proj(x2, norm_w, w_in, b_f, *, batch, seq):
    n, d = x2.shape
    tm = ATTN_TILE
    nblk = seq // tm
    h = N_HEADS
    q_scale = float(d // h) ** -0.5 * LOG2E
    wqT = w_in[:, 0:d].T.astype(BF16)
    wk = w_in[:, d:2 * d].astype(BF16)
    wvT = w_in[:, 2 * d:3 * d].T.astype(BF16)
    wf = jnp.pad(w_in[:, 3 * d:], ((0, 0), (0, LANES - h)))
    wf_hi = wf.astype(BF16)
    wf_lo = (wf - wf_hi.astype(F32)).astype(BF16)
    wf_cat = jnp.concatenate([wf_hi, wf_hi, wf_lo], axis=0)
    bf = jnp.pad(b_f, (0, LANES - h)).reshape(1, LANES)
    route = _kc_route(d)
    est = (2 * tm * d * 4 + 3 * d * d * 2 + wf_cat.size * 2 + route.size * 2
           + 4 * 2 * tm * d * 2 + 8 * tm * d * 4 + 2 * tm * tm * 2)
    row = lambda i: (i, 0)
    tblk = lambda i: (i // nblk, i % nblk, 0, 0)
    return pl.pallas_call(
        functools.partial(_attn_proj_kernel, q_scale=q_scale),
        grid=(n // tm,),
        in_specs=[pl.BlockSpec((tm, d), row), _resident((1, d)), _resident((d, d)),
                  _resident((d, d)), _resident((d, d)), _resident(wf_cat.shape),
                  _resident((1, LANES)), _resident(route.shape)],
        out_specs=[pl.BlockSpec((None, None, d, tm), tblk), pl.BlockSpec((tm, d), row),
                   pl.BlockSpec((None, None, d, tm), tblk), pl.BlockSpec((tm, d), row),
                   pl.BlockSpec((None, 1, LANES), lambda i: (i, 0, 0))],
        out_shape=[jax.ShapeDtypeStruct((batch, nblk, d, tm), BF16),
                   jax.ShapeDtypeStruct((n, d), BF16),
                   jax.ShapeDtypeStruct((batch, nblk, d, tm), BF16),
                   jax.ShapeDtypeStruct((n, d), BF16),
                   jax.ShapeDtypeStruct((n // tm, 1, LANES), F32)],
        compiler_params=pltpu.CompilerParams(dimension_semantics=("parallel",),
                                             vmem_limit_bytes=_vmem_limit(est)),
        name="attn_proj",
    )(x2, norm_w.reshape(1, d), wqT, wk, wvT, wf_cat, bf, route)


def _block_offset_kernel(tot_ref, off_ref):
    nblk = tot_ref.shape[0]
    r_i = lax.broadcasted_iota(jnp.int32, (nblk, nblk), 0)
    c_i = lax.broadcasted_iota(jnp.int32, (nblk, nblk), 1)
    strict_lower = (c_i < r_i).astype(BF16)
    lower3 = jnp.concatenate([strict_lower] * 3, axis=1)
    parts = jnp.concatenate(_split3(tot_ref[...]), axis=0)
    off_ref[...] = _dot(lower3, parts) * LOG2E


def _block_offsets(tot, *, batch):
    nb = tot.shape[0] // batch
    tot3 = tot.reshape(batch, nb, LANES)
    blk = pl.BlockSpec((None, nb, LANES), lambda b: (b, 0, 0))
    return pl.pallas_call(
        _block_offset_kernel,
        grid=(batch,),
        in_specs=[blk],
        out_specs=blk,
        out_shape=jax.ShapeDtypeStruct(tot3.shape, F32),
        compiler_params=pltpu.CompilerParams(dimension_semantics=("parallel",)),
        name="logf_block_offsets",
    )(tot3)


def _attn_kernel(off_ref, qT_ref, k_ref, kc_ref, vT_ref, o_ref, m_scr, acc_scr,
                 s0_scr, s1_scr, bm0_scr, bm1_scr, *, blk, dh, nblk):
    b = pl.program_id(0)
    hp = pl.program_id(1)
    i = pl.program_id(2)
    qT = qT_ref[...]
    row = lax.broadcasted_iota(jnp.int32, qT.shape, 0)
    zero_half = jnp.zeros((dh, blk), BF16)
    w_aug = []
    for hh in range(2):
        q_rows = [zero_half, zero_half]
        q_rows[hh] = qT[hh * dh:(hh + 1) * dh, :]
        ones_rows = jnp.where((row >= 3 * hh) & (row < 3 * hh + 3), 1.0, 0.0).astype(BF16)
        w_aug.append(jnp.concatenate(q_rows + [ones_rows], axis=0))
    ones_v = jnp.ones((BF16_SUBLANES, blk), BF16)
    off_base = [((b * N_HEADS) + 2 * hp + hh) * nblk for hh in range(2)]
    off_i = [jnp.full((1, blk), off_ref[off_base[hh] + i], F32) for hh in range(2)]

    m_scr[...] = jnp.full(m_scr.shape, -jnp.inf, F32)
    acc_scr[...] = jnp.zeros(acc_scr.shape, F32)

    def scores(j, s_scr, bm_scr):
        ka = jnp.concatenate([k_ref[j], kc_ref[j]], axis=1)
        for hh in range(2):
            sT = _dot(ka, w_aug[hh])
            s_scr[hh] = sT
            bm_scr[hh] = jnp.max(sT, axis=0, keepdims=True)

    def softmax_pv(j, s_scr, bm_scr, masked):
        vT = vT_ref[j]
        ps, alphas = [], []
        for hh in range(2):
            sT = s_scr[hh]
            if masked:
                k_i = lax.broadcasted_iota(jnp.int32, sT.shape, 0)
                q_i = lax.broadcasted_iota(jnp.int32, sT.shape, 1)
                sT = jnp.where(k_i <= q_i, sT, -jnp.inf)
                bm = jnp.max(sT, axis=0, keepdims=True)
            else:
                bm = bm_scr[hh]
            d = off_i[hh] - jnp.full((1, blk), off_ref[off_base[hh] + j], F32)
            m_prev = m_scr[hh]
            m_new = jnp.maximum(m_prev, bm + d)
            alphas.append(jnp.exp2(m_prev - m_new))
            ps.append(jnp.exp2(sT - (m_new - d)).astype(BF16))
            m_scr[hh] = m_new
        for hh in range(2):
            v_aug = jnp.concatenate([vT[hh * dh:(hh + 1) * dh, :], ones_v], axis=0)
            acc_scr[hh] = alphas[hh] * acc_scr[hh] + _dot(v_aug, ps[hh])

    scores(0, s0_scr, bm0_scr)

    def pair_body(t, carry):
        j = 2 * t
        scores(j + 1, s1_scr, bm1_scr)
        softmax_pv(j, s0_scr, bm0_scr, masked=False)
        scores(j + 2, s0_scr, bm0_scr)
        softmax_pv(j + 1, s1_scr, bm1_scr, masked=False)
        return carry

    lax.fori_loop(0, i // 2, pair_body, 0)

    @pl.when(i % 2 == 0)
    def _():
        softmax_pv(i, s0_scr, bm0_scr, masked=True)

    @pl.when(i % 2 == 1)
    def _():
        scores(i, s1_scr, bm1_scr)
        softmax_pv(i - 1, s0_scr, bm0_scr, masked=False)
        softmax_pv(i, s1_scr, bm1_scr, masked=True)

    outs = []
    for hh in range(2):
        acc = acc_scr[hh]
        outs.append(acc[:dh, :] / acc[dh:dh + 1, :])
    o_ref[...] = jnp.transpose(jnp.concatenate(outs, axis=0)).astype(o_ref.dtype)


def _attention(qT, k, kc, vT, off, *, batch, seq):
    n, d = k.shape
    blk = ATTN_TILE
    dh = d // N_HEADS
    pair = 2 * dh
    assert pair == LANES and seq % blk == 0 and dh % BF16_SUBLANES == 0
    nblk = seq // blk
    k4 = k.reshape(batch, nblk, blk, d)
    kc4 = kc.reshape(batch, nblk, blk, d)
    off_flat = off[:, :, :N_HEADS].transpose(0, 2, 1).reshape(batch * N_HEADS * nblk)
    acc_rows = dh + BF16_SUBLANES
    est = (2 * 4 * seq * pair * 2 + 4 * blk * pair * 2 + 2 * acc_rows * blk * 4
           + 10 * blk * blk * 4)
    resident = lambda b, hp, i: (b, 0, 0, hp)
    out = pl.pallas_call(
        functools.partial(_attn_kernel, blk=blk, dh=dh, nblk=nblk),
        grid=(batch, N_HEADS // 2, nblk),
        in_specs=[pl.BlockSpec(memory_space=pltpu.SMEM),
                  pl.BlockSpec((None, None, pair, blk), lambda b, hp, i: (b, i, hp, 0)),
                  pl.BlockSpec((None, nblk, blk, pair), resident),
                  pl.BlockSpec((None, nblk, blk, pair), resident),
                  pl.BlockSpec((None, nblk, pair, blk), lambda b, hp, i: (b, 0, hp, 0))],
        out_specs=pl.BlockSpec((None, blk, pair), lambda b, hp, i: (b, i, hp)),
        out_shape=jax.ShapeDtypeStruct((batch, seq, d), BF16),
        scratch_shapes=[pltpu.VMEM((2, 1, blk), F32), pltpu.VMEM((2, acc_rows, blk), F32),
                        pltpu.VMEM((2, blk, blk), F32), pltpu.VMEM((2, blk, blk), F32),
                        pltpu.VMEM((2, 1, blk), F32), pltpu.VMEM((2, 1, blk), F32)],
        compiler_params=pltpu.CompilerParams(
            dimension_semantics=("parallel", "parallel", "arbitrary"),
            vmem_limit_bytes=_vmem_limit(est)),
        name="fox_attention",
    )(off_flat, qT, k4, kc4, vT)
    return out.reshape(n, d)


def _sgu_kernel(x_ref, nw_ref, win_ref, lng_ref, lnb_ref, ws_ref, bs_ref, o_ref, v_scr, vn_scr,
                *, width):
    tm = x_ref.shape[0]
    cw = SGU_COL_CHUNK
    n_pos_chunks = tm // SGU_CHUNK
    gdim = width // SGU_GROUPS
    hb = _rms_norm(x_ref[...], nw_ref[...]).astype(BF16)

    for c in range(width // cw):
        v_scr[:, c * cw:(c + 1) * cw] = _gelu(_dot(hb, win_ref[:, width + c * cw:width + (c + 1) * cw]))
    v = v_scr[...]
    mu = jnp.mean(v, axis=-1, keepdims=True)
    dv = v - mu
    var = jnp.mean(dv * dv, axis=-1, keepdims=True)
    vn_scr[...] = (dv * lax.rsqrt(var + LN_EPS) * lng_ref[...] + lnb_ref[...]).astype(BF16)

    r_i = lax.broadcasted_iota(jnp.int32, (SGU_CHUNK, SGU_CHUNK), 0)
    c_i = lax.broadcasted_iota(jnp.int32, (SGU_CHUNK, SGU_CHUNK), 1)
    causal = c_i <= r_i
    for c in range(width // cw):
        u = _gelu(_dot(hb, win_ref[:, c * cw:(c + 1) * cw]))
        for gg in range(cw // gdim):
            g = c * (cw // gdim) + gg
            w_tril = jnp.where(causal, ws_ref[g], 0.0).astype(BF16)
            cols = slice(g * gdim, (g + 1) * gdim)
            v_cat = jnp.concatenate(
                [vn_scr[pc * SGU_CHUNK:(pc + 1) * SGU_CHUNK, cols] for pc in range(n_pos_chunks)],
                axis=1)
            mixed = _dot(w_tril, v_cat) + bs_ref[g]
            for pc in range(n_pos_chunks):
                rows = slice(pc * SGU_CHUNK, (pc + 1) * SGU_CHUNK)
                gated = u[rows, gg * gdim:(gg + 1) * gdim] * mixed[:, pc * gdim:(pc + 1) * gdim]
                o_ref[rows, cols] = gated.astype(o_ref.dtype)


def _sgu(x2, norm_w, w_in, ln_g, ln_b, w_s, b_s, *, tm):
    n, d = x2.shape
    width = w_in.shape[1] // 2
    assert width // SGU_GROUPS == LANES and w_s.shape[1] == SGU_CHUNK and tm % SGU_CHUNK == 0
    win = w_in.astype(BF16)
    est = (2 * tm * d * 4 + win.size * 2 + 2 * w_s.size * 4 + 2 * tm * width * 2
           + tm * width * 4 + tm * width * 2 + 4 * tm * width * 4)
    row = lambda i: (i, 0)
    return pl.pallas_call(
        functools.partial(_sgu_kernel, width=width),
        grid=(n // tm,),
        in_specs=[pl.BlockSpec((tm, d), row), _resident((1, d)), _resident(win.shape),
                  _resident((1, width)), _resident((1, width)), _resident(w_s.shape),
                  _resident((SGU_GROUPS, SGU_CHUNK, 1))],
        out_specs=pl.BlockSpec((tm, width), row),
        out_shape=jax.ShapeDtypeStruct((n, width), BF16),
        scratch_shapes=[pltpu.VMEM((tm, width), F32), pltpu.VMEM((tm, width), BF16)],
        compiler_params=pltpu.CompilerParams(dimension_semantics=("parallel",),
                                             vmem_limit_bytes=_vmem_limit(est)),
        name="sgu_gate",
    )(x2, norm_w.reshape(1, d), win, ln_g.reshape(1, width), ln_b.reshape(1, width), w_s,
      b_s.reshape(SGU_GROUPS, SGU_CHUNK, 1))


def _post_kernel(x_ref, a_ref, wa_ref, nw_ref, win_ref, wout_ref, fw_ref, o_ref, *, hidden, final):
    xm = x_ref[...] + _dot(a_ref[...], wa_ref[...])
    hb = _rms_norm(xm, nw_ref[...]).astype(BF16)
    o_ref[...] = xm
    th = FFN_CHUNK
    for c in range(hidden // th):
        g = _dot(hb, win_ref[:, c * th:(c + 1) * th])
        u = _dot(hb, win_ref[:, hidden + c * th:hidden + (c + 1) * th])
        act = (g * (1.0 / (1.0 + jnp.exp(-g))) * u).astype(BF16)
        o_ref[...] += _dot(act, wout_ref[c * th:(c + 1) * th, :])
    if final:
        o_ref[...] = _rms_norm(o_ref[...], fw_ref[...])


def _post(x2, a, w_a, norm_w, w_in, w_out, final_w, *, tm, final):
    n, d = x2.shape
    ka = a.shape[1]
    hidden = w_out.shape[0]
    assert hidden % FFN_CHUNK == 0
    wa, win, wout = w_a.astype(BF16), w_in.astype(BF16), w_out.astype(BF16)
    est = (2 * tm * d * 4 + 2 * tm * ka * 2 + (wa.size + win.size + wout.size) * 2
           + 2 * tm * d * 4 + 4 * tm * d * 4)
    row = lambda i: (i, 0)
    return pl.pallas_call(
        functools.partial(_post_kernel, hidden=hidden, final=final),
        grid=(n // tm,),
        in_specs=[pl.BlockSpec((tm, d), row), pl.BlockSpec((tm, ka), row), _resident(wa.shape),
                  _resident((1, d)), _resident(win.shape), _resident(wout.shape),
                  _resident((1, d))],
        out_specs=pl.BlockSpec((tm, d), row),
        out_shape=jax.ShapeDtypeStruct((n, d), F32),
        compiler_params=pltpu.CompilerParams(dimension_semantics=("parallel",),
                                             vmem_limit_bytes=_vmem_limit(est)),
        name="mixer_out_ffn_final" if final else "mixer_out_ffn",
    )(x2, a, wa, norm_w.reshape(1, d), win, wout, final_w.reshape(1, d))


def kernel(x, mixer_norm_w, attn_w_in, attn_b_f, attn_w_out, sgu_w_in, sgu_ln_g, sgu_ln_b,
           sgu_w_s, sgu_b_s, sgu_w_out, ffn_norm_w, ffn_w_in, ffn_w_out, final_norm_w):
    batch, seq, d = x.shape
    n = batch * seq
    depth = mixer_norm_w.shape[0]
    tm = ROW_TILE
    assert n % tm == 0 and seq % ATTN_TILE == 0
    x2 = x.reshape(n, d)
    for i in range(depth):
        j = i // 2
        if i % 2 == 0:
            qT, k, vT, kc, tot = _attn_proj(x2, mixer_norm_w[i], attn_w_in[j], attn_b_f[j],
                                            batch=batch, seq=seq)
            off = _block_offsets(tot, batch=batch)
            a = _attention(qT, k, kc, vT, off, batch=batch, seq=seq)
            w_a = attn_w_out[j]
        else:
            a = _sgu(x2, mixer_norm_w[i], sgu_w_in[j], sgu_ln_g[j], sgu_ln_b[j], sgu_w_s[j],
                     sgu_b_s[j], tm=tm)
            w_a = sgu_w_out[j]
        x2 = _post(x2, a, w_a, ffn_norm_w[i], ffn_w_in[i], ffn_w_out[i], final_norm_w,
                   tm=tm, final=(i == depth - 1))
    return x2.reshape(batch, seq, d)
```

```python
import functools
import math

import jax
import jax.numpy as jnp
from jax import lax
from jax.experimental import pallas as pl
from jax.experimental.pallas import tpu as pltpu

N_HEADS = 16
SGU_CHUNK = 128
SGU_GROUPS = 16
NORM_EPS = 1e-6
LN_EPS = 1e-5

LANES = 128
BF16_SUBLANES = 16
VMEM_LIMIT_CAP = 60000 * 1024

ROW_TILE = 512
ATTN_TILE = 512
FFN_CHUNK = 256
SGU_COL_CHUNK = 512

LOG2E = math.log2(math.e)
EXP2_ZERO_GAP = 152.0
BOUND_SLACK = 1.0 + 2.0 ** -10
F32 = jnp.float32
BF16 = jnp.bfloat16
NT_DIMS = (((1,), (1,)), ((), ()))


def _vmem_limit(nbytes):
    return int(min(VMEM_LIMIT_CAP, max(16 * 1024 * 1024, nbytes * 3 // 2)))


def _resident(shape):
    zeros = (0,) * len(shape)
    return pl.BlockSpec(shape, lambda *_: zeros, pipeline_mode=pl.Buffered(1))


def _dot(a, b):
    return jnp.dot(a, b, preferred_element_type=F32)


def _rms_norm(xf, w):
    ms = jnp.mean(xf * xf, axis=-1, keepdims=True)
    return xf * lax.rsqrt(ms + NORM_EPS) * w


def _gelu(z):
    return 0.5 * z * (1.0 + lax.erf(z * (1.0 / math.sqrt(2.0))))


def _split3(x):
    hi = x.astype(BF16)
    r1 = x - hi.astype(F32)
    mid = r1.astype(BF16)
    lo = (r1 - mid.astype(F32)).astype(BF16)
    return hi, mid, lo


def _attn_proj_kernel(x_ref, nw_ref, wqT_ref, wk_ref, wvT_ref, wf_ref, bf_ref, route_ref,
                      qT_ref, k_ref, vT_ref, kc_ref, tot_ref, kabs_ref, *, q_scale):
    tm = x_ref.shape[0]
    h = _rms_norm(x_ref[...], nw_ref[...])
    hb = h.astype(BF16)
    h_lo = (h - hb.astype(F32)).astype(BF16)
    hcat = jnp.concatenate([hb, h_lo, hb], axis=1)
    z = _dot(hcat, wf_ref[...]) + bf_ref[...]

    qT = lax.dot_general(wqT_ref[...], hb, NT_DIMS, preferred_element_type=F32)
    qT_ref[...] = (qT * q_scale).astype(BF16)

    logf = jnp.minimum(z, 0.0) - jnp.log1p(jnp.exp(-jnp.abs(z)))
    r_i = lax.broadcasted_iota(jnp.int32, (tm, tm), 0)
    c_i = lax.broadcasted_iota(jnp.int32, (tm, tm), 1)
    lower = (c_i <= r_i).astype(BF16)
    w3 = _dot(lower, jnp.concatenate(_split3(logf), axis=1))

    kb = _dot(hb, wk_ref[...]).astype(BF16)
    k_ref[...] = kb
    kabs_ref[...] = jnp.max(jnp.abs(kb.astype(F32)), axis=0, keepdims=True)

    w = w3[:, :LANES] + w3[:, LANES:2 * LANES] + w3[:, 2 * LANES:]
    tot_ref[...] = w[tm - 1:tm, :]
    bias = jnp.concatenate(_split3(w * (-LOG2E)), axis=1)
    kc_ref[...] = _dot(bias, route_ref[...]).astype(BF16)

    vT = lax.dot_general(wvT_ref[...], hb, NT_DIMS, preferred_element_type=F32)
    vT_ref[...] = vT.astype(BF16)


def _kc_route(d):
    part = jnp.arange(3 * LANES) // LANES
    head = jnp.arange(3 * LANES) % LANES
    target = (head // 2) * LANES + 3 * (head % 2) + part
    valid = head < N_HEADS
    cols = jnp.arange(d)
    return ((cols[None, :] == target[:, None]) & valid[:, None]).astype(BF16)


def _attn_proj(x2, norm_w, w_in, b_f, head_order, *, batch, seq):
    n, d = x2.shape
    tm = ATTN_TILE
    nblk = seq // tm
    h = N_HEADS
    q_scale = float(d // h) ** -0.5 * LOG2E
    by_head = lambda w: w.reshape(d, h, d // h)[:, head_order, :].reshape(d, d)
    wqT = by_head(w_in[:, 0:d]).T.astype(BF16)
    wk = by_head(w_in[:, d:2 * d]).astype(BF16)
    wvT = by_head(w_in[:, 2 * d:3 * d]).T.astype(BF16)
    b_f = b_f[head_order]
    wf = jnp.pad(w_in[:, 3 * d:][:, head_order], ((0, 0), (0, LANES - h)))
    wf_hi = wf.astype(BF16)
    wf_lo = (wf - wf_hi.astype(F32)).astype(BF16)
    wf_cat = jnp.concatenate([wf_hi, wf_hi, wf_lo], axis=0)
    bf = jnp.pad(b_f, (0, LANES - h)).reshape(1, LANES)
    route = _kc_route(d)
    est = (2 * tm * d * 4 + 3 * d * d * 2 + wf_cat.size * 2 + route.size * 2
           + 4 * 2 * tm * d * 2 + 8 * tm * d * 4 + 2 * tm * tm * 2)
    row = lambda i: (i, 0)
    tblk = lambda i: (i // nblk, i % nblk, 0, 0)
    return pl.pallas_call(
        functools.partial(_attn_proj_kernel, q_scale=q_scale),
        grid=(n // tm,),
        in_specs=[pl.BlockSpec((tm, d), row), _resident((1, d)), _resident((d, d)),
                  _resident((d, d)), _resident((d, d)), _resident(wf_cat.shape),
                  _resident((1, LANES)), _resident(route.shape)],
        out_specs=[pl.BlockSpec((None, None, d, tm), tblk), pl.BlockSpec((tm, d), row),
                   pl.BlockSpec((None, None, d, tm), tblk), pl.BlockSpec((tm, d), row),
                   pl.BlockSpec((None, 1, LANES), lambda i: (i, 0, 0)),
                   pl.BlockSpec((None, 1, d), lambda i: (i, 0, 0))],
        out_shape=[jax.ShapeDtypeStruct((batch, nblk, d, tm), BF16),
                   jax.ShapeDtypeStruct((n, d), BF16),
                   jax.ShapeDtypeStruct((batch, nblk, d, tm), BF16),
                   jax.ShapeDtypeStruct((n, d), BF16),
                   jax.ShapeDtypeStruct((n // tm, 1, LANES), F32),
                   jax.ShapeDtypeStruct((n // tm, 1, d), F32)],
        compiler_params=pltpu.CompilerParams(dimension_semantics=("parallel",),
                                             vmem_limit_bytes=_vmem_limit(est)),
        name="attn_proj",
    )(x2, norm_w.reshape(1, d), wqT, wk, wvT, wf_cat, bf, route)


def _block_offset_kernel(tot_ref, off_ref):
    nblk = tot_ref.shape[0]
    r_i = lax.broadcasted_iota(jnp.int32, (nblk, nblk), 0)
    c_i = lax.broadcasted_iota(jnp.int32, (nblk, nblk), 1)
    strict_lower = (c_i < r_i).astype(BF16)
    lower3 = jnp.concatenate([strict_lower] * 3, axis=1)
    parts = jnp.concatenate(_split3(tot_ref[...]), axis=0)
    off_ref[...] = _dot(lower3, parts) * LOG2E


def _block_offsets(tot, *, batch):
    nb = tot.shape[0] // batch
    tot3 = tot.reshape(batch, nb, LANES)
    blk = pl.BlockSpec((None, nb, LANES), lambda b: (b, 0, 0))
    return pl.pallas_call(
        _block_offset_kernel,
        grid=(batch,),
        in_specs=[blk],
        out_specs=blk,
        out_shape=jax.ShapeDtypeStruct(tot3.shape, F32),
        compiler_params=pltpu.CompilerParams(dimension_semantics=("parallel",)),
        name="logf_block_offsets",
    )(tot3)


def _first_needed_block(off_ref, off_base, off_i, qT, kabs, i, *, blk, dh, nblk):
    qabs = jnp.abs(qT)
    zero_half = jnp.zeros((dh, blk), BF16)
    blk_idx = lax.broadcasted_iota(jnp.int32, (nblk, 1), 0)
    kabs_b = kabs.astype(BF16)
    lo = None
    for hh in range(2):
        q_rows = [zero_half, zero_half]
        q_rows[hh] = qabs[hh * dh:(hh + 1) * dh, :]
        bounds = _dot(kabs_b, jnp.concatenate(q_rows, axis=0))
        ub = jnp.max(bounds, axis=1, keepdims=True) * BOUND_SLACK
        ub_all = jnp.max(jnp.where(blk_idx <= i, ub, 0.0), axis=0, keepdims=True)
        ub_diag = jnp.max(jnp.where(blk_idx == i, ub, 0.0), axis=0, keepdims=True)
        off_next = jnp.zeros((nblk, 1), F32)
        for j in range(nblk - 1):
            off_next = jnp.where(blk_idx == j, off_ref[off_base[hh] + j + 1], off_next)
        gap = -(ub_all + ub_diag + (off_i[hh][:, 0:1] - off_next))
        skip = (blk_idx < i) & (gap > EXP2_ZERO_GAP)
        lo_h = jnp.sum(skip.astype(jnp.int32))
        lo = lo_h if lo is None else jnp.minimum(lo, lo_h)
    return lo


def _attn_kernel(off_ref, qT_ref, k_ref, kc_ref, vT_ref, kabs_ref, o_ref, m_scr, acc_scr,
                 s0_scr, s1_scr, bm0_scr, bm1_scr, *, blk, dh, nblk):
    b = pl.program_id(0)
    hp = pl.program_id(1)
    i = pl.program_id(2)
    qT = qT_ref[...]
    row = lax.broadcasted_iota(jnp.int32, qT.shape, 0)
    zero_half = jnp.zeros((dh, blk), BF16)
    w_aug = []
    for hh in range(2):
        q_rows = [zero_half, zero_half]
        q_rows[hh] = qT[hh * dh:(hh + 1) * dh, :]
        ones_rows = jnp.where((row >= 3 * hh) & (row < 3 * hh + 3), 1.0, 0.0).astype(BF16)
        w_aug.append(jnp.concatenate(q_rows + [ones_rows], axis=0))
    ones_v = jnp.ones((BF16_SUBLANES, blk), BF16)
    off_base = [((b * N_HEADS) + 2 * hp + hh) * nblk for hh in range(2)]
    off_i = [jnp.full((1, blk), off_ref[off_base[hh] + i], F32) for hh in range(2)]

    m_scr[...] = jnp.full(m_scr.shape, -jnp.inf, F32)
    acc_scr[...] = jnp.zeros(acc_scr.shape, F32)

    def scores(j, s_scr, bm_scr):
        ka = jnp.concatenate([k_ref[j], kc_ref[j]], axis=1)
        for hh in range(2):
            sT = _dot(ka, w_aug[hh])
            s_scr[hh] = sT
            bm_scr[hh] = jnp.max(sT, axis=0, keepdims=True)

    def softmax(j, s_scr, bm_scr, masked):
        ps, alphas = [], []
        for hh in range(2):
            sT = s_scr[hh]
            if masked:
                k_i = lax.broadcasted_iota(jnp.int32, sT.shape, 0)
                q_i = lax.broadcasted_iota(jnp.int32, sT.shape, 1)
                sT = jnp.where(k_i <= q_i, sT, -jnp.inf)
                bm = jnp.max(sT, axis=0, keepdims=True)
            else:
                bm = bm_scr[hh]
            d = off_i[hh] - jnp.full((1, blk), off_ref[off_base[hh] + j], F32)
            m_prev = m_scr[hh]
            m_new = jnp.maximum(m_prev, bm + d)
            alphas.append(jnp.exp2(m_prev - m_new))
            ps.append(jnp.exp2(sT - (m_new - d)).astype(BF16))
            m_scr[hh] = m_new
        return ps, alphas

    def pv(j, ps, alphas):
        vT = vT_ref[j]
        for hh in range(2):
            v_aug = jnp.concatenate([vT[hh * dh:(hh + 1) * dh, :], ones_v], axis=0)
            acc_scr[hh] = alphas[hh] * acc_scr[hh] + _dot(v_aug, ps[hh])

    def softmax_pv(j, s_scr, bm_scr, masked):
        pv(j, *softmax(j, s_scr, bm_scr, masked))

    lo = _first_needed_block(off_ref, off_base, off_i, qT, kabs_ref[...], i,
                             blk=blk, dh=dh, nblk=nblk)
    scores(lo, s0_scr, bm0_scr)

    def pair_body(t, carry):
        j = lo + 2 * t
        probs = softmax(j, s0_scr, bm0_scr, masked=False)
        scores(j + 1, s1_scr, bm1_scr)
        pv(j, *probs)
        probs = softmax(j + 1, s1_scr, bm1_scr, masked=False)
        scores(j + 2, s0_scr, bm0_scr)
        pv(j + 1, *probs)
        return carry

    lax.fori_loop(0, (i - lo) // 2, pair_body, 0)

    @pl.when((i - lo) % 2 == 0)
    def _():
        softmax_pv(i, s0_scr, bm0_scr, masked=True)

    @pl.when((i - lo) % 2 == 1)
    def _():
        scores(i, s1_scr, bm1_scr)
        softmax_pv(i - 1, s0_scr, bm0_scr, masked=False)
        softmax_pv(i, s1_scr, bm1_scr, masked=True)

    outs = []
    for hh in range(2):
        acc = acc_scr[hh]
        outs.append(acc[:dh, :] / acc[dh:dh + 1, :])
    o_ref[...] = jnp.transpose(jnp.concatenate(outs, axis=0)).astype(o_ref.dtype)


def _attention(qT, k, kc, vT, kabs, off, *, batch, seq):
    n, d = k.shape
    blk = ATTN_TILE
    dh = d // N_HEADS
    pair = 2 * dh
    assert pair == LANES and seq % blk == 0 and dh % BF16_SUBLANES == 0
    nblk = seq // blk
    k4 = k.reshape(batch, nblk, blk, d)
    kc4 = kc.reshape(batch, nblk, blk, d)
    kabs3 = kabs.reshape(batch, nblk, d)
    off_flat = off[:, :, :N_HEADS].transpose(0, 2, 1).reshape(batch * N_HEADS * nblk)
    acc_rows = dh + BF16_SUBLANES
    est = (2 * 4 * seq * pair * 2 + 4 * blk * pair * 2 + 2 * acc_rows * blk * 4
           + 10 * blk * blk * 4)
    resident = lambda b, hp, i: (b, 0, 0, hp)
    out = pl.pallas_call(
        functools.partial(_attn_kernel, blk=blk, dh=dh, nblk=nblk),
        grid=(batch, N_HEADS // 2, nblk),
        in_specs=[pl.BlockSpec(memory_space=pltpu.SMEM),
                  pl.BlockSpec((None, None, pair, blk), lambda b, hp, i: (b, i, hp, 0)),
                  pl.BlockSpec((None, nblk, blk, pair), resident),
                  pl.BlockSpec((None, nblk, blk, pair), resident),
                  pl.BlockSpec((None, nblk, pair, blk), lambda b, hp, i: (b, 0, hp, 0)),
                  pl.BlockSpec((None, nblk, pair), lambda b, hp, i: (b, 0, hp))],
        out_specs=pl.BlockSpec((None, blk, pair), lambda b, hp, i: (b, i, hp)),
        out_shape=jax.ShapeDtypeStruct((batch, seq, d), BF16),
        scratch_shapes=[pltpu.VMEM((2, 1, blk), F32), pltpu.VMEM((2, acc_rows, blk), F32),
                        pltpu.VMEM((2, blk, blk), F32), pltpu.VMEM((2, blk, blk), F32),
                        pltpu.VMEM((2, 1, blk), F32), pltpu.VMEM((2, 1, blk), F32)],
        compiler_params=pltpu.CompilerParams(
            dimension_semantics=("parallel", "parallel", "arbitrary"),
            vmem_limit_bytes=_vmem_limit(est)),
        name="fox_attention",
    )(off_flat, qT, k4, kc4, vT, kabs3)
    return out.reshape(n, d)


def _sgu_kernel(x_ref, nw_ref, win_ref, lng_ref, lnb_ref, ws_ref, bs_ref, o_ref, v_scr, vn_scr,
                *, width):
    tm = x_ref.shape[0]
    cw = SGU_COL_CHUNK
    n_pos_chunks = tm // SGU_CHUNK
    gdim = width // SGU_GROUPS
    hb = _rms_norm(x_ref[...], nw_ref[...]).astype(BF16)

    for c in range(width // cw):
        v_scr[:, c * cw:(c + 1) * cw] = _gelu(_dot(hb, win_ref[:, width + c * cw:width + (c + 1) * cw]))
    v = v_scr[...]
    mu = jnp.mean(v, axis=-1, keepdims=True)
    dv = v - mu
    var = jnp.mean(dv * dv, axis=-1, keepdims=True)
    vn_scr[...] = (dv * lax.rsqrt(var + LN_EPS) * lng_ref[...] + lnb_ref[...]).astype(BF16)

    r_i = lax.broadcasted_iota(jnp.int32, (SGU_CHUNK, SGU_CHUNK), 0)
    c_i = lax.broadcasted_iota(jnp.int32, (SGU_CHUNK, SGU_CHUNK), 1)
    causal = c_i <= r_i
    for c in range(width // cw):
        u = _gelu(_dot(hb, win_ref[:, c * cw:(c + 1) * cw]))
        for gg in range(cw // gdim):
            g = c * (cw // gdim) + gg
            w_tril = jnp.where(causal, ws_ref[g], 0.0).astype(BF16)
            cols = slice(g * gdim, (g + 1) * gdim)
            v_cat = jnp.concatenate(
                [vn_scr[pc * SGU_CHUNK:(pc + 1) * SGU_CHUNK, cols] for pc in range(n_pos_chunks)],
                axis=1)
            mixed = _dot(w_tril, v_cat) + bs_ref[g]
            for pc in range(n_pos_chunks):
                rows = slice(pc * SGU_CHUNK, (pc + 1) * SGU_CHUNK)
                gated = u[rows, gg * gdim:(gg + 1) * gdim] * mixed[:, pc * gdim:(pc + 1) * gdim]
                o_ref[rows, cols] = gated.astype(o_ref.dtype)


def _sgu(x2, norm_w, w_in, ln_g, ln_b, w_s, b_s, *, tm):
    n, d = x2.shape
    width = w_in.shape[1] // 2
    assert width // SGU_GROUPS == LANES and w_s.shape[1] == SGU_CHUNK and tm % SGU_CHUNK == 0
    win = w_in.astype(BF16)
    est = (2 * tm * d * 4 + win.size * 2 + 2 * w_s.size * 4 + 2 * tm * width * 2
           + tm * width * 4 + tm * width * 2 + 4 * tm * width * 4)
    row = lambda i: (i, 0)
    return pl.pallas_call(
        functools.partial(_sgu_kernel, width=width),
        grid=(n // tm,),
        in_specs=[pl.BlockSpec((tm, d), row), _resident((1, d)), _resident(win.shape),
                  _resident((1, width)), _resident((1, width)), _resident(w_s.shape),
                  _resident((SGU_GROUPS, SGU_CHUNK, 1))],
        out_specs=pl.BlockSpec((tm, width), row),
        out_shape=jax.ShapeDtypeStruct((n, width), BF16),
        scratch_shapes=[pltpu.VMEM((tm, width), F32), pltpu.VMEM((tm, width), BF16)],
        compiler_params=pltpu.CompilerParams(dimension_semantics=("parallel",),
                                             vmem_limit_bytes=_vmem_limit(est)),
        name="sgu_gate",
    )(x2, norm_w.reshape(1, d), win, ln_g.reshape(1, width), ln_b.reshape(1, width), w_s,
      b_s.reshape(SGU_GROUPS, SGU_CHUNK, 1))


def _post_kernel(x_ref, a_ref, wa_ref, nw_ref, win_ref, wout_ref, fw_ref, o_ref, *, hidden, final):
    xm = x_ref[...] + _dot(a_ref[...], wa_ref[...])
    hb = _rms_norm(xm, nw_ref[...]).astype(BF16)
    o_ref[...] = xm
    th = FFN_CHUNK
    for c in range(hidden // th):
        g = _dot(hb, win_ref[:, c * th:(c + 1) * th])
        u = _dot(hb, win_ref[:, hidden + c * th:hidden + (c + 1) * th])
        act = (g * (1.0 / (1.0 + jnp.exp(-g))) * u).astype(BF16)
        o_ref[...] += _dot(act, wout_ref[c * th:(c + 1) * th, :])
    if final:
        o_ref[...] = _rms_norm(o_ref[...], fw_ref[...])


def _post(x2, a, w_a, norm_w, w_in, w_out, final_w, *, tm, final):
    n, d = x2.shape
    ka = a.shape[1]
    hidden = w_out.shape[0]
    assert hidden % FFN_CHUNK == 0
    wa, win, wout = w_a.astype(BF16), w_in.astype(BF16), w_out.astype(BF16)
    est = (2 * tm * d * 4 + 2 * tm * ka * 2 + (wa.size + win.size + wout.size) * 2
           + 2 * tm * d * 4 + 4 * tm * d * 4)
    row = lambda i: (i, 0)
    return pl.pallas_call(
        functools.partial(_post_kernel, hidden=hidden, final=final),
        grid=(n // tm,),
        in_specs=[pl.BlockSpec((tm, d), row), pl.BlockSpec((tm, ka), row), _resident(wa.shape),
                  _resident((1, d)), _resident(win.shape), _resident(wout.shape),
                  _resident((1, d))],
        out_specs=pl.BlockSpec((tm, d), row),
        out_shape=jax.ShapeDtypeStruct((n, d), F32),
        compiler_params=pltpu.CompilerParams(dimension_semantics=("parallel",),
                                             vmem_limit_bytes=_vmem_limit(est)),
        name="mixer_out_ffn_final" if final else "mixer_out_ffn",
    )(x2, a, wa, norm_w.reshape(1, d), win, wout, final_w.reshape(1, d))


def kernel(x, mixer_norm_w, attn_w_in, attn_b_f, attn_w_out, sgu_w_in, sgu_ln_g, sgu_ln_b,
           sgu_w_s, sgu_b_s, sgu_w_out, ffn_norm_w, ffn_w_in, ffn_w_out, final_norm_w):
    batch, seq, d = x.shape
    n = batch * seq
    depth = mixer_norm_w.shape[0]
    tm = ROW_TILE
    assert n % tm == 0 and seq % ATTN_TILE == 0
    x2 = x.reshape(n, d)
    for i in range(depth):
        j = i // 2
        if i % 2 == 0:
            head_order = jnp.argsort(attn_b_f[j])
            qT, k, vT, kc, tot, kabs = _attn_proj(x2, mixer_norm_w[i], attn_w_in[j], attn_b_f[j],
                                                  head_order, batch=batch, seq=seq)
            off = _block_offsets(tot, batch=batch)
            a = _attention(qT, k, kc, vT, kabs, off, batch=batch, seq=seq)
            w_a = attn_w_out[j].reshape(N_HEADS, d // N_HEADS, d)[head_order].reshape(d, d)
        else:
            a = _sgu(x2, mixer_norm_w[i], sgu_w_in[j], sgu_ln_g[j], sgu_ln_b[j], sgu_w_s[j],
                     sgu_b_s[j], tm=tm)
            w_a = sgu_w_out[j]
        x2 = _post(x2, a, w_a, ffn_norm_w[i], ffn_w_in[i], ffn_w_out[i], final_norm_w,
                   tm=tm, final=(i == depth - 1))
    return x2.reshape(batch, seq, d)
```

```python
import functools
import math

import jax
import jax.numpy as jnp
from jax import lax
from jax.experimental import pallas as pl
from jax.experimental.pallas import tpu as pltpu

N_HEADS = 16
SGU_CHUNK = 128
SGU_GROUPS = 16
NORM_EPS = 1e-6
LN_EPS = 1e-5

LANES = 128
BF16_SUBLANES = 16
VMEM_LIMIT_CAP = 60000 * 1024

ROW_TILE = 512
POST_ROW_TILE = 1024
ATTN_TILE = 512
ATTN_UNROLLS = (8, 4, 2)
FFN_CHUNK = 256
SGU_COL_CHUNK = 512

LOG2E = math.log2(math.e)
EXP2_ZERO_GAP = 152.0
BOUND_SLACK = 1.0 + 2.0 ** -10
F32 = jnp.float32
BF16 = jnp.bfloat16
NT_DIMS = (((1,), (1,)), ((), ()))


def _vmem_limit(nbytes):
    return int(min(VMEM_LIMIT_CAP, max(16 * 1024 * 1024, nbytes * 3 // 2)))


def _resident(shape):
    zeros = (0,) * len(shape)
    return pl.BlockSpec(shape, lambda *_: zeros, pipeline_mode=pl.Buffered(1))


def _dot(a, b):
    return jnp.dot(a, b, preferred_element_type=F32)


def _rms_norm(xf, w):
    ms = jnp.mean(xf * xf, axis=-1, keepdims=True)
    return xf * lax.rsqrt(ms + NORM_EPS) * w


def _gelu(z):
    return 0.5 * z * (1.0 + lax.erf(z * (1.0 / math.sqrt(2.0))))


def _split3(x):
    hi = x.astype(BF16)
    r1 = x - hi.astype(F32)
    mid = r1.astype(BF16)
    lo = (r1 - mid.astype(F32)).astype(BF16)
    return hi, mid, lo


def _attn_proj_kernel(x_ref, nw_ref, wqT_ref, wk_ref, wvT_ref, wf_ref, bf_ref, route_ref,
                      qT_ref, k_ref, vT_ref, kc_ref, tot_ref, kabs_ref, *, q_scale):
    tm = x_ref.shape[0]
    h = _rms_norm(x_ref[...], nw_ref[...])
    hb = h.astype(BF16)
    h_lo = (h - hb.astype(F32)).astype(BF16)
    hcat = jnp.concatenate([hb, h_lo, hb], axis=1)
    z = _dot(hcat, wf_ref[...]) + bf_ref[...]

    qT = lax.dot_general(wqT_ref[...], hb, NT_DIMS, preferred_element_type=F32)
    qT_ref[...] = (qT * q_scale).astype(BF16)

    logf = jnp.minimum(z, 0.0) - jnp.log1p(jnp.exp(-jnp.abs(z)))
    r_i = lax.broadcasted_iota(jnp.int32, (tm, tm), 0)
    c_i = lax.broadcasted_iota(jnp.int32, (tm, tm), 1)
    lower = (c_i <= r_i).astype(BF16)
    w3 = _dot(lower, jnp.concatenate(_split3(logf), axis=1))

    kb = _dot(hb, wk_ref[...]).astype(BF16)
    k_ref[...] = kb
    kabs_ref[...] = jnp.max(jnp.abs(kb.astype(F32)), axis=0, keepdims=True)

    w = w3[:, :LANES] + w3[:, LANES:2 * LANES] + w3[:, 2 * LANES:]
    tot_ref[...] = w[tm - 1:tm, :]
    bias = jnp.concatenate(_split3(w * (-LOG2E)), axis=1)
    kc_ref[...] = _dot(bias, route_ref[...]).astype(BF16)

    vT = lax.dot_general(wvT_ref[...], hb, NT_DIMS, preferred_element_type=F32)
    vT_ref[...] = vT.astype(BF16)


def _kc_route(d):
    part = jnp.arange(3 * LANES) // LANES
    head = jnp.arange(3 * LANES) % LANES
    target = (head // 2) * LANES + 3 * (head % 2) + part
    valid = head < N_HEADS
    cols = jnp.arange(d)
    return ((cols[None, :] == target[:, None]) & valid[:, None]).astype(BF16)


def _attn_proj(x2, norm_w, w_in, b_f, head_order, *, batch, seq):
    n, d = x2.shape
    tm = ATTN_TILE
    nblk = seq // tm
    h = N_HEADS
    q_scale = float(d // h) ** -0.5 * LOG2E
    by_head = lambda w: w.reshape(d, h, d // h)[:, head_order, :].reshape(d, d)
    wqT = by_head(w_in[:, 0:d]).T.astype(BF16)
    wk = by_head(w_in[:, d:2 * d]).astype(BF16)
    wvT = by_head(w_in[:, 2 * d:3 * d]).T.astype(BF16)
    b_f = b_f[head_order]
    wf = jnp.pad(w_in[:, 3 * d:][:, head_order], ((0, 0), (0, LANES - h)))
    wf_hi = wf.astype(BF16)
    wf_lo = (wf - wf_hi.astype(F32)).astype(BF16)
    wf_cat = jnp.concatenate([wf_hi, wf_hi, wf_lo], axis=0)
    bf = jnp.pad(b_f, (0, LANES - h)).reshape(1, LANES)
    route = _kc_route(d)
    est = (2 * tm * d * 4 + 3 * d * d * 2 + wf_cat.size * 2 + route.size * 2
           + 4 * 2 * tm * d * 2 + 8 * tm * d * 4 + 2 * tm * tm * 2)
    row = lambda i: (i, 0)
    tblk = lambda i: (i // nblk, i % nblk, 0, 0)
    return pl.pallas_call(
        functools.partial(_attn_proj_kernel, q_scale=q_scale),
        grid=(n // tm,),
        in_specs=[pl.BlockSpec((tm, d), row), _resident((1, d)), _resident((d, d)),
                  _resident((d, d)), _resident((d, d)), _resident(wf_cat.shape),
                  _resident((1, LANES)), _resident(route.shape)],
        out_specs=[pl.BlockSpec((None, None, d, tm), tblk), pl.BlockSpec((tm, d), row),
                   pl.BlockSpec((None, None, d, tm), tblk), pl.BlockSpec((tm, d), row),
                   pl.BlockSpec((None, 1, LANES), lambda i: (i, 0, 0)),
                   pl.BlockSpec((None, 1, d), lambda i: (i, 0, 0))],
        out_shape=[jax.ShapeDtypeStruct((batch, nblk, d, tm), BF16),
                   jax.ShapeDtypeStruct((n, d), BF16),
                   jax.ShapeDtypeStruct((batch, nblk, d, tm), BF16),
                   jax.ShapeDtypeStruct((n, d), BF16),
                   jax.ShapeDtypeStruct((n // tm, 1, LANES), F32),
                   jax.ShapeDtypeStruct((n // tm, 1, d), F32)],
        compiler_params=pltpu.CompilerParams(dimension_semantics=("parallel",),
                                             vmem_limit_bytes=_vmem_limit(est)),
        name="attn_proj",
    )(x2, norm_w.reshape(1, d), wqT, wk, wvT, wf_cat, bf, route)


def _block_offset_kernel(tot_ref, off_ref):
    nblk = tot_ref.shape[0]
    r_i = lax.broadcasted_iota(jnp.int32, (nblk, nblk), 0)
    c_i = lax.broadcasted_iota(jnp.int32, (nblk, nblk), 1)
    strict_lower = (c_i < r_i).astype(BF16)
    lower3 = jnp.concatenate([strict_lower] * 3, axis=1)
    parts = jnp.concatenate(_split3(tot_ref[...]), axis=0)
    off_ref[...] = _dot(lower3, parts) * LOG2E


def _block_offsets(tot, *, batch):
    nb = tot.shape[0] // batch
    tot3 = tot.reshape(batch, nb, LANES)
    blk = pl.BlockSpec((None, nb, LANES), lambda b: (b, 0, 0))
    return pl.pallas_call(
        _block_offset_kernel,
        grid=(batch,),
        in_specs=[blk],
        out_specs=blk,
        out_shape=jax.ShapeDtypeStruct(tot3.shape, F32),
        compiler_params=pltpu.CompilerParams(dimension_semantics=("parallel",)),
        name="logf_block_offsets",
    )(tot3)


def _first_needed_block(off_ref, off_base, off_i, qT, kabs, i, *, blk, dh, nblk):
    qabs = jnp.abs(qT)
    zero_half = jnp.zeros((dh, blk), BF16)
    blk_idx = lax.broadcasted_iota(jnp.int32, (nblk, 1), 0)
    kabs_b = kabs.astype(BF16)
    lo = None
    for hh in range(2):
        q_rows = [zero_half, zero_half]
        q_rows[hh] = qabs[hh * dh:(hh + 1) * dh, :]
        bounds = _dot(kabs_b, jnp.concatenate(q_rows, axis=0))
        ub = jnp.max(bounds, axis=1, keepdims=True) * BOUND_SLACK
        ub_all = jnp.max(jnp.where(blk_idx <= i, ub, 0.0), axis=0, keepdims=True)
        ub_diag = jnp.max(jnp.where(blk_idx == i, ub, 0.0), axis=0, keepdims=True)
        off_next = jnp.zeros((nblk, 1), F32)
        for j in range(nblk - 1):
            off_next = jnp.where(blk_idx == j, off_ref[off_base[hh] + j + 1], off_next)
        gap = -(ub_all + ub_diag + (off_i[hh][:, 0:1] - off_next))
        skip = (blk_idx < i) & (gap > EXP2_ZERO_GAP)
        lo_h = jnp.sum(skip.astype(jnp.int32))
        lo = lo_h if lo is None else jnp.minimum(lo, lo_h)
    return lo


def _attn_kernel(off_ref, qT_ref, k_ref, kc_ref, vT_ref, kabs_ref, o_ref, m_scr, acc_scr,
                 s0_scr, s1_scr, bm0_scr, bm1_scr, *, blk, dh, nblk):
    b = pl.program_id(0)
    hp = pl.program_id(1)
    i = pl.program_id(2)
    qT = qT_ref[...]
    row = lax.broadcasted_iota(jnp.int32, qT.shape, 0)
    zero_half = jnp.zeros((dh, blk), BF16)
    w_aug = []
    for hh in range(2):
        q_rows = [zero_half, zero_half]
        q_rows[hh] = qT[hh * dh:(hh + 1) * dh, :]
        ones_rows = jnp.where((row >= 3 * hh) & (row < 3 * hh + 3), 1.0, 0.0).astype(BF16)
        w_aug.append(jnp.concatenate(q_rows + [ones_rows], axis=0))
    ones_v = jnp.ones((BF16_SUBLANES, blk), BF16)
    off_base = [((b * N_HEADS) + 2 * hp + hh) * nblk for hh in range(2)]
    off_i = [jnp.full((1, blk), off_ref[off_base[hh] + i], F32) for hh in range(2)]

    m_scr[...] = jnp.full(m_scr.shape, -jnp.inf, F32)
    acc_scr[...] = jnp.zeros(acc_scr.shape, F32)

    def scores(j, s_scr, bm_scr):
        ka = jnp.concatenate([k_ref[j], kc_ref[j]], axis=1)
        for hh in range(2):
            sT = _dot(ka, w_aug[hh])
            s_scr[hh] = sT
            bm_scr[hh] = jnp.max(sT, axis=0, keepdims=True)

    def softmax(j, s_scr, bm_scr, masked):
        ps, alphas = [], []
        for hh in range(2):
            sT = s_scr[hh]
            if masked:
                k_i = lax.broadcasted_iota(jnp.int32, sT.shape, 0)
                q_i = lax.broadcasted_iota(jnp.int32, sT.shape, 1)
                sT = jnp.where(k_i <= q_i, sT, -jnp.inf)
                bm = jnp.max(sT, axis=0, keepdims=True)
            else:
                bm = bm_scr[hh]
            d = off_i[hh] - jnp.full((1, blk), off_ref[off_base[hh] + j], F32)
            m_prev = m_scr[hh]
            m_new = jnp.maximum(m_prev, bm + d)
            alphas.append(jnp.exp2(m_prev - m_new))
            ps.append(jnp.exp2(sT - (m_new - d)).astype(BF16))
            m_scr[hh] = m_new
        return ps, alphas

    def pv(j, ps, alphas):
        vT = vT_ref[j]
        for hh in range(2):
            v_aug = jnp.concatenate([vT[hh * dh:(hh + 1) * dh, :], ones_v], axis=0)
            acc_scr[hh] = alphas[hh] * acc_scr[hh] + _dot(v_aug, ps[hh])

    def softmax_pv(j, s_scr, bm_scr, masked):
        pv(j, *softmax(j, s_scr, bm_scr, masked))

    lo = _first_needed_block(off_ref, off_base, off_i, qT, kabs_ref[...], i,
                             blk=blk, dh=dh, nblk=nblk)
    scores(lo, s0_scr, bm0_scr)
    bufs = ((s0_scr, bm0_scr), (s1_scr, bm1_scr))

    def steps(j, count):
        for u in range(count):
            probs = softmax(j + u, *bufs[u % 2], masked=False)
            scores(j + u + 1, *bufs[(u + 1) % 2])
            pv(j + u, *probs)

    start, left = lo, i - lo
    for unroll in ATTN_UNROLLS:
        def body(t, carry, start=start, unroll=unroll):
            steps(start + unroll * t, unroll)
            return carry

        trips = left // unroll
        lax.fori_loop(0, trips, body, 0)
        start, left = start + unroll * trips, left - unroll * trips

    @pl.when((i - lo) % 2 == 0)
    def _():
        softmax_pv(i, s0_scr, bm0_scr, masked=True)

    @pl.when((i - lo) % 2 == 1)
    def _():
        scores(i, s1_scr, bm1_scr)
        softmax_pv(i - 1, s0_scr, bm0_scr, masked=False)
        softmax_pv(i, s1_scr, bm1_scr, masked=True)

    outs = []
    for hh in range(2):
        acc = acc_scr[hh]
        outs.append(acc[:dh, :] / acc[dh:dh + 1, :])
    o_ref[...] = jnp.transpose(jnp.concatenate(outs, axis=0)).astype(o_ref.dtype)


def _attention(qT, k, kc, vT, kabs, off, *, batch, seq):
    n, d = k.shape
    blk = ATTN_TILE
    dh = d // N_HEADS
    pair = 2 * dh
    assert pair == LANES and seq % blk == 0 and dh % BF16_SUBLANES == 0
    nblk = seq // blk
    k4 = k.reshape(batch, nblk, blk, d)
    kc4 = kc.reshape(batch, nblk, blk, d)
    kabs3 = kabs.reshape(batch, nblk, d)
    off_flat = off[:, :, :N_HEADS].transpose(0, 2, 1).reshape(batch * N_HEADS * nblk)
    acc_rows = dh + BF16_SUBLANES
    est = (2 * 4 * seq * pair * 2 + 4 * blk * pair * 2 + 2 * acc_rows * blk * 4
           + 10 * blk * blk * 4)
    resident = lambda b, hp, i: (b, 0, 0, hp)
    out = pl.pallas_call(
        functools.partial(_attn_kernel, blk=blk, dh=dh, nblk=nblk),
        grid=(batch, N_HEADS // 2, nblk),
        in_specs=[pl.BlockSpec(memory_space=pltpu.SMEM),
                  pl.BlockSpec((None, None, pair, blk), lambda b, hp, i: (b, i, hp, 0)),
                  pl.BlockSpec((None, nblk, blk, pair), resident),
                  pl.BlockSpec((None, nblk, blk, pair), resident),
                  pl.BlockSpec((None, nblk, pair, blk), lambda b, hp, i: (b, 0, hp, 0)),
                  pl.BlockSpec((None, nblk, pair), lambda b, hp, i: (b, 0, hp))],
        out_specs=pl.BlockSpec((None, blk, pair), lambda b, hp, i: (b, i, hp)),
        out_shape=jax.ShapeDtypeStruct((batch, seq, d), BF16),
        scratch_shapes=[pltpu.VMEM((2, 1, blk), F32), pltpu.VMEM((2, acc_rows, blk), F32),
                        pltpu.VMEM((2, blk, blk), F32), pltpu.VMEM((2, blk, blk), F32),
                        pltpu.VMEM((2, 1, blk), F32), pltpu.VMEM((2, 1, blk), F32)],
        compiler_params=pltpu.CompilerParams(
            dimension_semantics=("parallel", "parallel", "arbitrary"),
            vmem_limit_bytes=_vmem_limit(est)),
        name="fox_attention",
    )(off_flat, qT, k4, kc4, vT, kabs3)
    return out.reshape(n, d)


def _sgu_kernel(x_ref, nw_ref, win_ref, lng_ref, lnb_ref, ws_ref, bs_ref, o_ref, v_scr, vn_scr,
                *, width):
    tm = x_ref.shape[0]
    cw = SGU_COL_CHUNK
    n_pos_chunks = tm // SGU_CHUNK
    gdim = width // SGU_GROUPS
    hb = _rms_norm(x_ref[...], nw_ref[...]).astype(BF16)

    for c in range(width // cw):
        v_scr[:, c * cw:(c + 1) * cw] = _gelu(_dot(hb, win_ref[:, width + c * cw:width + (c + 1) * cw]))
    v = v_scr[...]
    mu = jnp.mean(v, axis=-1, keepdims=True)
    dv = v - mu
    var = jnp.mean(dv * dv, axis=-1, keepdims=True)
    vn_scr[...] = (dv * lax.rsqrt(var + LN_EPS) * lng_ref[...] + lnb_ref[...]).astype(BF16)

    r_i = lax.broadcasted_iota(jnp.int32, (SGU_CHUNK, SGU_CHUNK), 0)
    c_i = lax.broadcasted_iota(jnp.int32, (SGU_CHUNK, SGU_CHUNK), 1)
    causal = c_i <= r_i
    for c in range(width // cw):
        u = _gelu(_dot(hb, win_ref[:, c * cw:(c + 1) * cw]))
        for gg in range(cw // gdim):
            g = c * (cw // gdim) + gg
            w_tril = jnp.where(causal, ws_ref[g], 0.0).astype(BF16)
            cols = slice(g * gdim, (g + 1) * gdim)
            v_cat = jnp.concatenate(
                [vn_scr[pc * SGU_CHUNK:(pc + 1) * SGU_CHUNK, cols] for pc in range(n_pos_chunks)],
                axis=1)
            mixed = _dot(w_tril, v_cat) + bs_ref[g]
            for pc in range(n_pos_chunks):
                rows = slice(pc * SGU_CHUNK, (pc + 1) * SGU_CHUNK)
                gated = u[rows, gg * gdim:(gg + 1) * gdim] * mixed[:, pc * gdim:(pc + 1) * gdim]
                o_ref[rows, cols] = gated.astype(o_ref.dtype)


def _sgu(x2, norm_w, w_in, ln_g, ln_b, w_s, b_s, *, tm):
    n, d = x2.shape
    width = w_in.shape[1] // 2
    assert width // SGU_GROUPS == LANES and w_s.shape[1] == SGU_CHUNK and tm % SGU_CHUNK == 0
    win = w_in.astype(BF16)
    est = (2 * tm * d * 4 + win.size * 2 + 2 * w_s.size * 4 + 2 * tm * width * 2
           + tm * width * 4 + tm * width * 2 + 4 * tm * width * 4)
    row = lambda i: (i, 0)
    return pl.pallas_call(
        functools.partial(_sgu_kernel, width=width),
        grid=(n // tm,),
        in_specs=[pl.BlockSpec((tm, d), row), _resident((1, d)), _resident(win.shape),
                  _resident((1, width)), _resident((1, width)), _resident(w_s.shape),
                  _resident((SGU_GROUPS, SGU_CHUNK, 1))],
        out_specs=pl.BlockSpec((tm, width), row),
        out_shape=jax.ShapeDtypeStruct((n, width), BF16),
        scratch_shapes=[pltpu.VMEM((tm, width), F32), pltpu.VMEM((tm, width), BF16)],
        compiler_params=pltpu.CompilerParams(dimension_semantics=("parallel",),
                                             vmem_limit_bytes=_vmem_limit(est)),
        name="sgu_gate",
    )(x2, norm_w.reshape(1, d), win, ln_g.reshape(1, width), ln_b.reshape(1, width), w_s,
      b_s.reshape(SGU_GROUPS, SGU_CHUNK, 1))


def _post_kernel(x_ref, a_ref, wa_ref, nw_ref, win_ref, wout_ref, fw_ref, o_ref, *, hidden, final):
    xm = x_ref[...] + _dot(a_ref[...], wa_ref[...])
    hb = _rms_norm(xm, nw_ref[...]).astype(BF16)
    o_ref[...] = xm
    th = FFN_CHUNK
    for c in range(hidden // th):
        g = _dot(hb, win_ref[:, c * th:(c + 1) * th])
        u = _dot(hb, win_ref[:, hidden + c * th:hidden + (c + 1) * th])
        act = (g * (1.0 / (1.0 + jnp.exp(-g))) * u).astype(BF16)
        o_ref[...] += _dot(act, wout_ref[c * th:(c + 1) * th, :])
    if final:
        o_ref[...] = _rms_norm(o_ref[...], fw_ref[...])


def _post(x2, a, w_a, norm_w, w_in, w_out, final_w, *, tm, final):
    n, d = x2.shape
    ka = a.shape[1]
    hidden = w_out.shape[0]
    assert hidden % FFN_CHUNK == 0
    wa, win, wout = w_a.astype(BF16), w_in.astype(BF16), w_out.astype(BF16)
    est = (2 * tm * d * 4 + 2 * tm * ka * 2 + (wa.size + win.size + wout.size) * 2
           + 2 * tm * d * 4 + 4 * tm * d * 4)
    row = lambda i: (i, 0)
    return pl.pallas_call(
        functools.partial(_post_kernel, hidden=hidden, final=final),
        grid=(n // tm,),
        in_specs=[pl.BlockSpec((tm, d), row), pl.BlockSpec((tm, ka), row), _resident(wa.shape),
                  _resident((1, d)), _resident(win.shape), _resident(wout.shape),
                  _resident((1, d))],
        out_specs=pl.BlockSpec((tm, d), row),
        out_shape=jax.ShapeDtypeStruct((n, d), F32),
        compiler_params=pltpu.CompilerParams(dimension_semantics=("parallel",),
                                             vmem_limit_bytes=_vmem_limit(est)),
        name="mixer_out_ffn_final" if final else "mixer_out_ffn",
    )(x2, a, wa, norm_w.reshape(1, d), win, wout, final_w.reshape(1, d))


def kernel(x, mixer_norm_w, attn_w_in, attn_b_f, attn_w_out, sgu_w_in, sgu_ln_g, sgu_ln_b,
           sgu_w_s, sgu_b_s, sgu_w_out, ffn_norm_w, ffn_w_in, ffn_w_out, final_norm_w):
    batch, seq, d = x.shape
    n = batch * seq
    depth = mixer_norm_w.shape[0]
    tm = ROW_TILE
    assert n % tm == 0 and n % POST_ROW_TILE == 0 and seq % ATTN_TILE == 0
    x2 = x.reshape(n, d)
    for i in range(depth):
        j = i // 2
        if i % 2 == 0:
            head_order = jnp.argsort(attn_b_f[j])
            qT, k, vT, kc, tot, kabs = _attn_proj(x2, mixer_norm_w[i], attn_w_in[j], attn_b_f[j],
                                                  head_order, batch=batch, seq=seq)
            off = _block_offsets(tot, batch=batch)
            a = _attention(qT, k, kc, vT, kabs, off, batch=batch, seq=seq)
            w_a = attn_w_out[j].reshape(N_HEADS, d // N_HEADS, d)[head_order].reshape(d, d)
        else:
            a = _sgu(x2, mixer_norm_w[i], sgu_w_in[j], sgu_ln_g[j], sgu_ln_b[j], sgu_w_s[j],
                     sgu_b_s[j], tm=tm)
            w_a = sgu_w_out[j]
        x2 = _post(x2, a, w_a, ffn_norm_w[i], ffn_w_in[i], ffn_w_out[i], final_norm_w,
                   tm=POST_ROW_TILE, final=(i == depth - 1))
    return x2.reshape(batch, seq, d)
```

```python
import functools
import math

import jax
import jax.numpy as jnp
from jax import lax
from jax.experimental import pallas as pl
from jax.experimental.pallas import tpu as pltpu

N_HEADS = 16
SGU_CHUNK = 128
SGU_GROUPS = 16
NORM_EPS = 1e-6
LN_EPS = 1e-5

LANES = 128
BF16_SUBLANES = 16
VMEM_LIMIT_CAP = 60000 * 1024

ROW_TILE = 512
POST_ROW_TILE = 1024
ATTN_TILE = 512
ATTN_UNROLLS = (8, 4, 2)
FFN_CHUNK = 256
SGU_COL_CHUNK = 512

LOG2E = math.log2(math.e)
EXP2_ZERO_GAP = 152.0
BOUND_SLACK = 1.0 + 2.0 ** -10
F32 = jnp.float32
BF16 = jnp.bfloat16
NT_DIMS = (((1,), (1,)), ((), ()))


def _vmem_limit(nbytes):
    return int(min(VMEM_LIMIT_CAP, max(16 * 1024 * 1024, nbytes * 3 // 2)))


def _resident(shape):
    zeros = (0,) * len(shape)
    return pl.BlockSpec(shape, lambda *_: zeros, pipeline_mode=pl.Buffered(1))


def _dot(a, b):
    return jnp.dot(a, b, preferred_element_type=F32)


def _rms_norm(xf, w):
    ms = jnp.mean(xf * xf, axis=-1, keepdims=True)
    return xf * lax.rsqrt(ms + NORM_EPS) * w


def _gelu(z):
    return 0.5 * z * (1.0 + lax.erf(z * (1.0 / math.sqrt(2.0))))


def _split3(x):
    hi = x.astype(BF16)
    r1 = x - hi.astype(F32)
    mid = r1.astype(BF16)
    lo = (r1 - mid.astype(F32)).astype(BF16)
    return hi, mid, lo


def _attn_proj_kernel(x_ref, nw_ref, wqT_ref, wk_ref, wvT_ref, wf_ref, bf_ref, route_ref,
                      qT_ref, k_ref, vT_ref, kc_ref, tot_ref, kabs_ref, *, q_scale):
    tm = x_ref.shape[0]
    h = _rms_norm(x_ref[...], nw_ref[...])
    hb = h.astype(BF16)
    h_lo = (h - hb.astype(F32)).astype(BF16)
    hcat = jnp.concatenate([hb, h_lo, hb], axis=1)
    z = _dot(hcat, wf_ref[...]) + bf_ref[...]

    qT = lax.dot_general(wqT_ref[...], hb, NT_DIMS, preferred_element_type=F32)
    qT_ref[...] = (qT * q_scale).astype(BF16)

    logf = jnp.minimum(z, 0.0) - jnp.log1p(jnp.exp(-jnp.abs(z)))
    r_i = lax.broadcasted_iota(jnp.int32, (tm, tm), 0)
    c_i = lax.broadcasted_iota(jnp.int32, (tm, tm), 1)
    lower = (c_i <= r_i).astype(BF16)
    w3 = _dot(lower, jnp.concatenate(_split3(logf), axis=1))

    kb = _dot(hb, wk_ref[...]).astype(BF16)
    k_ref[...] = kb
    kabs_ref[...] = jnp.max(jnp.abs(kb.astype(F32)), axis=0, keepdims=True)

    w = w3[:, :LANES] + w3[:, LANES:2 * LANES] + w3[:, 2 * LANES:]
    tot_ref[...] = w[tm - 1:tm, :]
    bias = jnp.concatenate(_split3(w * (-LOG2E)), axis=1)
    kc_ref[...] = _dot(bias, route_ref[...]).astype(BF16)

    vT = lax.dot_general(wvT_ref[...], hb, NT_DIMS, preferred_element_type=F32)
    vT_ref[...] = vT.astype(BF16)


def _kc_route(d):
    part = jnp.arange(3 * LANES) // LANES
    head = jnp.arange(3 * LANES) % LANES
    target = (head // 2) * LANES + 3 * (head % 2) + part
    valid = head < N_HEADS
    cols = jnp.arange(d)
    return ((cols[None, :] == target[:, None]) & valid[:, None]).astype(BF16)


def _attn_proj(x2, norm_w, w_in, b_f, head_order, *, batch, seq):
    n, d = x2.shape
    tm = ATTN_TILE
    nblk = seq // tm
    h = N_HEADS
    q_scale = float(d // h) ** -0.5 * LOG2E
    by_head = lambda w: w.reshape(d, h, d // h)[:, head_order, :].reshape(d, d)
    wqT = by_head(w_in[:, 0:d]).T.astype(BF16)
    wk = by_head(w_in[:, d:2 * d]).astype(BF16)
    wvT = by_head(w_in[:, 2 * d:3 * d]).T.astype(BF16)
    b_f = b_f[head_order]
    wf = jnp.pad(w_in[:, 3 * d:][:, head_order], ((0, 0), (0, LANES - h)))
    wf_hi = wf.astype(BF16)
    wf_lo = (wf - wf_hi.astype(F32)).astype(BF16)
    wf_cat = jnp.concatenate([wf_hi, wf_hi, wf_lo], axis=0)
    bf = jnp.pad(b_f, (0, LANES - h)).reshape(1, LANES)
    route = _kc_route(d)
    est = (2 * tm * d * 4 + 3 * d * d * 2 + wf_cat.size * 2 + route.size * 2
           + 4 * 2 * tm * d * 2 + 8 * tm * d * 4 + 2 * tm * tm * 2)
    row = lambda i: (i, 0)
    tblk = lambda i: (i // nblk, i % nblk, 0, 0)
    return pl.pallas_call(
        functools.partial(_attn_proj_kernel, q_scale=q_scale),
        grid=(n // tm,),
        in_specs=[pl.BlockSpec((tm, d), row), _resident((1, d)), _resident((d, d)),
                  _resident((d, d)), _resident((d, d)), _resident(wf_cat.shape),
                  _resident((1, LANES)), _resident(route.shape)],
        out_specs=[pl.BlockSpec((None, None, d, tm), tblk), pl.BlockSpec((tm, d), row),
                   pl.BlockSpec((None, None, d, tm), tblk), pl.BlockSpec((tm, d), row),
                   pl.BlockSpec((None, 1, LANES), lambda i: (i, 0, 0)),
                   pl.BlockSpec((None, 1, d), lambda i: (i, 0, 0))],
        out_shape=[jax.ShapeDtypeStruct((batch, nblk, d, tm), BF16),
                   jax.ShapeDtypeStruct((n, d), BF16),
                   jax.ShapeDtypeStruct((batch, nblk, d, tm), BF16),
                   jax.ShapeDtypeStruct((n, d), BF16),
                   jax.ShapeDtypeStruct((n // tm, 1, LANES), F32),
                   jax.ShapeDtypeStruct((n // tm, 1, d), F32)],
        compiler_params=pltpu.CompilerParams(dimension_semantics=("parallel",),
                                             vmem_limit_bytes=_vmem_limit(est)),
        name="attn_proj",
    )(x2, norm_w.reshape(1, d), wqT, wk, wvT, wf_cat, bf, route)


def _block_offset_kernel(tot_ref, off_ref):
    nblk = tot_ref.shape[0]
    r_i = lax.broadcasted_iota(jnp.int32, (nblk, nblk), 0)
    c_i = lax.broadcasted_iota(jnp.int32, (nblk, nblk), 1)
    strict_lower = (c_i < r_i).astype(BF16)
    lower3 = jnp.concatenate([strict_lower] * 3, axis=1)
    parts = jnp.concatenate(_split3(tot_ref[...]), axis=0)
    off_ref[...] = _dot(lower3, parts) * LOG2E


def _block_offsets(tot, *, batch):
    nb = tot.shape[0] // batch
    tot3 = tot.reshape(batch, nb, LANES)
    blk = pl.BlockSpec((None, nb, LANES), lambda b: (b, 0, 0))
    return pl.pallas_call(
        _block_offset_kernel,
        grid=(batch,),
        in_specs=[blk],
        out_specs=blk,
        out_shape=jax.ShapeDtypeStruct(tot3.shape, F32),
        compiler_params=pltpu.CompilerParams(dimension_semantics=("parallel",)),
        name="logf_block_offsets",
    )(tot3)


def _first_needed_block(off_ref, off_base, off_i, qT, kabs, i, *, blk, dh, nblk):
    qabs = jnp.abs(qT)
    zero_half = jnp.zeros((dh, blk), BF16)
    blk_idx = lax.broadcasted_iota(jnp.int32, (nblk, 1), 0)
    kabs_b = kabs.astype(BF16)
    lo = None
    for hh in range(2):
        q_rows = [zero_half, zero_half]
        q_rows[hh] = qabs[hh * dh:(hh + 1) * dh, :]
        bounds = _dot(kabs_b, jnp.concatenate(q_rows, axis=0))
        ub = jnp.max(bounds, axis=1, keepdims=True) * BOUND_SLACK
        ub_all = jnp.max(jnp.where(blk_idx <= i, ub, 0.0), axis=0, keepdims=True)
        ub_diag = jnp.max(jnp.where(blk_idx == i, ub, 0.0), axis=0, keepdims=True)
        off_next = jnp.zeros((nblk, 1), F32)
        for j in range(nblk - 1):
            off_next = jnp.where(blk_idx == j, off_ref[off_base[hh] + j + 1], off_next)
        gap = -(ub_all + ub_diag + (off_i[hh][:, 0:1] - off_next))
        skip = (blk_idx < i) & (gap > EXP2_ZERO_GAP)
        lo_h = jnp.sum(skip.astype(jnp.int32))
        lo = lo_h if lo is None else jnp.minimum(lo, lo_h)
    return lo


def _attn_kernel(off_ref, qT_ref, k_ref, kc_ref, vT_ref, kabs_ref, o_ref, m_scr, acc_scr,
                 s0_scr, s1_scr, bm0_scr, bm1_scr, *, blk, dh, nblk):
    b = pl.program_id(0)
    hp = pl.program_id(1)
    i = pl.program_id(2)
    qT = qT_ref[...]
    row = lax.broadcasted_iota(jnp.int32, qT.shape, 0)
    zero_half = jnp.zeros((dh, blk), BF16)
    w_aug = []
    for hh in range(2):
        q_rows = [zero_half, zero_half]
        q_rows[hh] = qT[hh * dh:(hh + 1) * dh, :]
        ones_rows = jnp.where((row >= 3 * hh) & (row < 3 * hh + 3), 1.0, 0.0).astype(BF16)
        w_aug.append(jnp.concatenate(q_rows + [ones_rows], axis=0))
    ones_v = jnp.ones((BF16_SUBLANES, blk), BF16)
    off_base = [((b * N_HEADS) + 2 * hp + hh) * nblk for hh in range(2)]
    off_i = [jnp.full((1, blk), off_ref[off_base[hh] + i], F32) for hh in range(2)]

    m_scr[...] = jnp.full(m_scr.shape, -jnp.inf, F32)
    acc_scr[...] = jnp.zeros(acc_scr.shape, F32)

    def scores(j, s_scr, bm_scr):
        ka = jnp.concatenate([k_ref[j], kc_ref[j]], axis=1)
        for hh in range(2):
            sT = _dot(ka, w_aug[hh])
            s_scr[hh] = sT
            bm_scr[hh] = jnp.max(sT, axis=0, keepdims=True)

    def softmax(j, s_scr, bm_scr, masked):
        ps, alphas = [], []
        for hh in range(2):
            sT = s_scr[hh]
            if masked:
                k_i = lax.broadcasted_iota(jnp.int32, sT.shape, 0)
                q_i = lax.broadcasted_iota(jnp.int32, sT.shape, 1)
                sT = jnp.where(k_i <= q_i, sT, -jnp.inf)
                bm = jnp.max(sT, axis=0, keepdims=True)
            else:
                bm = bm_scr[hh]
            d = off_i[hh] - jnp.full((1, blk), off_ref[off_base[hh] + j], F32)
            m_prev = m_scr[hh]
            m_new = jnp.maximum(m_prev, bm + d)
            alphas.append(jnp.exp2(m_prev - m_new))
            ps.append(jnp.exp2(sT - (m_new - d)).astype(BF16))
            m_scr[hh] = m_new
        return ps, alphas

    def pv(j, ps, alphas):
        vT = vT_ref[j]
        for hh in range(2):
            v_aug = jnp.concatenate([vT[hh * dh:(hh + 1) * dh, :], ones_v], axis=0)
            acc_scr[hh] = alphas[hh] * acc_scr[hh] + _dot(v_aug, ps[hh])

    def softmax_pv(j, s_scr, bm_scr, masked):
        pv(j, *softmax(j, s_scr, bm_scr, masked))

    bufs = ((s0_scr, bm0_scr), (s1_scr, bm1_scr))
    scores(i, *bufs[0])
    scores(jnp.maximum(i - 1, 0), *bufs[1])
    lo = _first_needed_block(off_ref, off_base, off_i, qT, kabs_ref[...], i,
                             blk=blk, dh=dh, nblk=nblk)
    softmax_pv(i, *bufs[0], masked=True)

    def steps_down(j, count):
        for u in range(count):
            probs = softmax(j - u, *bufs[(u + 1) % 2], masked=False)
            scores(j - u - 1, *bufs[u % 2])
            pv(j - u, *probs)

    n_below = i - lo
    start, left = i - 1, jnp.maximum(n_below - 1, 0)
    for unroll in ATTN_UNROLLS:
        def body(t, carry, start=start, unroll=unroll):
            steps_down(start - unroll * t, unroll)
            return carry

        trips = left // unroll
        lax.fori_loop(0, trips, body, 0)
        start, left = start - unroll * trips, left - unroll * trips

    @pl.when((n_below >= 1) & (left == 1))
    def _():
        steps_down(start, 1)
        softmax_pv(lo, *bufs[0], masked=False)

    @pl.when((n_below >= 1) & (left == 0))
    def _():
        softmax_pv(lo, *bufs[1], masked=False)

    outs = []
    for hh in range(2):
        acc = acc_scr[hh]
        outs.append(acc[:dh, :] / acc[dh:dh + 1, :])
    o_ref[...] = jnp.transpose(jnp.concatenate(outs, axis=0)).astype(o_ref.dtype)


def _attention(qT, k, kc, vT, kabs, off, *, batch, seq):
    n, d = k.shape
    blk = ATTN_TILE
    dh = d // N_HEADS
    pair = 2 * dh
    assert pair == LANES and seq % blk == 0 and dh % BF16_SUBLANES == 0
    nblk = seq // blk
    k4 = k.reshape(batch, nblk, blk, d)
    kc4 = kc.reshape(batch, nblk, blk, d)
    kabs3 = kabs.reshape(batch, nblk, d)
    off_flat = off[:, :, :N_HEADS].transpose(0, 2, 1).reshape(batch * N_HEADS * nblk)
    acc_rows = dh + BF16_SUBLANES
    est = (2 * 4 * seq * pair * 2 + 4 * blk * pair * 2 + 2 * acc_rows * blk * 4
           + 10 * blk * blk * 4)
    resident = lambda b, hp, i: (b, 0, 0, hp)
    out = pl.pallas_call(
        functools.partial(_attn_kernel, blk=blk, dh=dh, nblk=nblk),
        grid=(batch, N_HEADS // 2, nblk),
        in_specs=[pl.BlockSpec(memory_space=pltpu.SMEM),
                  pl.BlockSpec((None, None, pair, blk), lambda b, hp, i: (b, i, hp, 0)),
                  pl.BlockSpec((None, nblk, blk, pair), resident),
                  pl.BlockSpec((None, nblk, blk, pair), resident),
                  pl.BlockSpec((None, nblk, pair, blk), lambda b, hp, i: (b, 0, hp, 0)),
                  pl.BlockSpec((None, nblk, pair), lambda b, hp, i: (b, 0, hp))],
        out_specs=pl.BlockSpec((None, blk, pair), lambda b, hp, i: (b, i, hp)),
        out_shape=jax.ShapeDtypeStruct((batch, seq, d), BF16),
        scratch_shapes=[pltpu.VMEM((2, 1, blk), F32), pltpu.VMEM((2, acc_rows, blk), F32),
                        pltpu.VMEM((2, blk, blk), F32), pltpu.VMEM((2, blk, blk), F32),
                        pltpu.VMEM((2, 1, blk), F32), pltpu.VMEM((2, 1, blk), F32)],
        compiler_params=pltpu.CompilerParams(
            dimension_semantics=("parallel", "parallel", "arbitrary"),
            vmem_limit_bytes=_vmem_limit(est)),
        name="fox_attention",
    )(off_flat, qT, k4, kc4, vT, kabs3)
    return out.reshape(n, d)


def _sgu_kernel(x_ref, nw_ref, win_ref, lng_ref, lnb_ref, ws_ref, bs_ref, o_ref, v_scr, vn_scr,
                *, width):
    tm = x_ref.shape[0]
    cw = SGU_COL_CHUNK
    n_pos_chunks = tm // SGU_CHUNK
    gdim = width // SGU_GROUPS
    hb = _rms_norm(x_ref[...], nw_ref[...]).astype(BF16)

    for c in range(width // cw):
        v_scr[:, c * cw:(c + 1) * cw] = _gelu(_dot(hb, win_ref[:, width + c * cw:width + (c + 1) * cw]))
    v = v_scr[...]
    mu = jnp.mean(v, axis=-1, keepdims=True)
    dv = v - mu
    var = jnp.mean(dv * dv, axis=-1, keepdims=True)
    vn_scr[...] = (dv * lax.rsqrt(var + LN_EPS) * lng_ref[...] + lnb_ref[...]).astype(BF16)

    r_i = lax.broadcasted_iota(jnp.int32, (SGU_CHUNK, SGU_CHUNK), 0)
    c_i = lax.broadcasted_iota(jnp.int32, (SGU_CHUNK, SGU_CHUNK), 1)
    causal = c_i <= r_i
    for c in range(width // cw):
        u = _gelu(_dot(hb, win_ref[:, c * cw:(c + 1) * cw]))
        for gg in range(cw // gdim):
            g = c * (cw // gdim) + gg
            w_tril = jnp.where(causal, ws_ref[g], 0.0).astype(BF16)
            cols = slice(g * gdim, (g + 1) * gdim)
            v_cat = jnp.concatenate(
                [vn_scr[pc * SGU_CHUNK:(pc + 1) * SGU_CHUNK, cols] for pc in range(n_pos_chunks)],
                axis=1)
            mixed = _dot(w_tril, v_cat) + bs_ref[g]
            for pc in range(n_pos_chunks):
                rows = slice(pc * SGU_CHUNK, (pc + 1) * SGU_CHUNK)
                gated = u[rows, gg * gdim:(gg + 1) * gdim] * mixed[:, pc * gdim:(pc + 1) * gdim]
                o_ref[rows, cols] = gated.astype(o_ref.dtype)


def _sgu(x2, norm_w, w_in, ln_g, ln_b, w_s, b_s, *, tm):
    n, d = x2.shape
    width = w_in.shape[1] // 2
    assert width // SGU_GROUPS == LANES and w_s.shape[1] == SGU_CHUNK and tm % SGU_CHUNK == 0
    win = w_in.astype(BF16)
    est = (2 * tm * d * 4 + win.size * 2 + 2 * w_s.size * 4 + 2 * tm * width * 2
           + tm * width * 4 + tm * width * 2 + 4 * tm * width * 4)
    row = lambda i: (i, 0)
    return pl.pallas_call(
        functools.partial(_sgu_kernel, width=width),
        grid=(n // tm,),
        in_specs=[pl.BlockSpec((tm, d), row), _resident((1, d)), _resident(win.shape),
                  _resident((1, width)), _resident((1, width)), _resident(w_s.shape),
                  _resident((SGU_GROUPS, SGU_CHUNK, 1))],
        out_specs=pl.BlockSpec((tm, width), row),
        out_shape=jax.ShapeDtypeStruct((n, width), BF16),
        scratch_shapes=[pltpu.VMEM((tm, width), F32), pltpu.VMEM((tm, width), BF16)],
        compiler_params=pltpu.CompilerParams(dimension_semantics=("parallel",),
                                             vmem_limit_bytes=_vmem_limit(est)),
        name="sgu_gate",
    )(x2, norm_w.reshape(1, d), win, ln_g.reshape(1, width), ln_b.reshape(1, width), w_s,
      b_s.reshape(SGU_GROUPS, SGU_CHUNK, 1))


def _post_kernel(x_ref, a_ref, wa_ref, nw_ref, win_ref, wout_ref, fw_ref, o_ref, *, hidden, final):
    xm = x_ref[...] + _dot(a_ref[...], wa_ref[...])
    hb = _rms_norm(xm, nw_ref[...]).astype(BF16)
    o_ref[...] = xm
    th = FFN_CHUNK
    for c in range(hidden // th):
        g = _dot(hb, win_ref[:, c * th:(c + 1) * th])
        u = _dot(hb, win_ref[:, hidden + c * th:hidden + (c + 1) * th])
        act = (g * (1.0 / (1.0 + jnp.exp(-g))) * u).astype(BF16)
        o_ref[...] += _dot(act, wout_ref[c * th:(c + 1) * th, :])
    if final:
        o_ref[...] = _rms_norm(o_ref[...], fw_ref[...])


def _post(x2, a, w_a, norm_w, w_in, w_out, final_w, *, tm, final):
    n, d = x2.shape
    ka = a.shape[1]
    hidden = w_out.shape[0]
    assert hidden % FFN_CHUNK == 0
    wa, win, wout = w_a.astype(BF16), w_in.astype(BF16), w_out.astype(BF16)
    est = (2 * tm * d * 4 + 2 * tm * ka * 2 + (wa.size + win.size + wout.size) * 2
           + 2 * tm * d * 4 + 4 * tm * d * 4)
    row = lambda i: (i, 0)
    return pl.pallas_call(
        functools.partial(_post_kernel, hidden=hidden, final=final),
        grid=(n // tm,),
        in_specs=[pl.BlockSpec((tm, d), row), pl.BlockSpec((tm, ka), row), _resident(wa.shape),
                  _resident((1, d)), _resident(win.shape), _resident(wout.shape),
                  _resident((1, d))],
        out_specs=pl.BlockSpec((tm, d), row),
        out_shape=jax.ShapeDtypeStruct((n, d), F32),
        compiler_params=pltpu.CompilerParams(dimension_semantics=("parallel",),
                                             vmem_limit_bytes=_vmem_limit(est)),
        name="mixer_out_ffn_final" if final else "mixer_out_ffn",
    )(x2, a, wa, norm_w.reshape(1, d), win, wout, final_w.reshape(1, d))


def kernel(x, mixer_norm_w, attn_w_in, attn_b_f, attn_w_out, sgu_w_in, sgu_ln_g, sgu_ln_b,
           sgu_w_s, sgu_b_s, sgu_w_out, ffn_norm_w, ffn_w_in, ffn_w_out, final_norm_w):
    batch, seq, d = x.shape
    n = batch * seq
    depth = mixer_norm_w.shape[0]
    tm = ROW_TILE
    assert n % tm == 0 and n % POST_ROW_TILE == 0 and seq % ATTN_TILE == 0
    x2 = x.reshape(n, d)
    for i in range(depth):
        j = i // 2
        if i % 2 == 0:
            head_order = jnp.argsort(attn_b_f[j])
            qT, k, vT, kc, tot, kabs = _attn_proj(x2, mixer_norm_w[i], attn_w_in[j], attn_b_f[j],
                                                  head_order, batch=batch, seq=seq)
            off = _block_offsets(tot, batch=batch)
            a = _attention(qT, k, kc, vT, kabs, off, batch=batch, seq=seq)
            w_a = attn_w_out[j].reshape(N_HEADS, d // N_HEADS, d)[head_order].reshape(d, d)
        else:
            a = _sgu(x2, mixer_norm_w[i], sgu_w_in[j], sgu_ln_g[j], sgu_ln_b[j], sgu_w_s[j],
                     sgu_b_s[j], tm=tm)
            w_a = sgu_w_out[j]
        x2 = _post(x2, a, w_a, ffn_norm_w[i], ffn_w_in[i], ffn_w_out[i], final_norm_w,
                   tm=POST_ROW_TILE, final=(i == depth - 1))
    return x2.reshape(batch, seq, d)
```
